```python
import jax, jax.numpy as jnp
from jax import lax
import numpy as np

D_MODEL = 1024
BATCH = 16
SEQ = 256
DEPTH = 1
DEC_BATCH = 4
DEC_SEQ = 2048
PAST_LEN = 512

GRID_W = 64
N_HEADS = 16
HEAD_DIM = D_MODEL // N_HEADS
D_RWKV = N_HEADS * HEAD_DIM
D_POOL = D_MODEL // 2
N_POOL_GROUPS = 4
POOL_GROUP = D_POOL // N_POOL_GROUPS
POOL_OUT_GROUP = D_MODEL // N_POOL_GROUPS
POOL_WINDOWS = (2, 4, 8, 16)
D_FF = 4 * D_MODEL
DECAY_LORA = 64
AAA_LORA = 64
GATE_LORA = 128
N_DIR = 2
N_MOD = 6
D_IN = D_POOL + 3 * D_RWKV + 2 * D_MODEL
RMS_EPS = 1e-6
LNX_EPS = 64e-5
NORM_EPS = 1e-12

kernel_name = "hybrid_pool_rwkv7_diffusion_step"

LAYER_KEYS = ("w_ada", "b_ada", "g_norm1", "g_norm2", "w_in", "mu_rkv", "mu_wag",
              "w_dec0", "w_dec1", "w_dec2", "a0", "a1", "a2", "gate_w1", "gate_w2",
              "k_k", "k_a", "r_k", "ln_x_w", "ln_x_b", "pool_w", "pool_scale",
              "w_out", "w_ff1", "w_ff2")


def rmsnorm(x, g):
    xf = x.astype(jnp.float32)
    y = xf * lax.rsqrt(jnp.mean(xf * xf, axis=-1, keepdims=True) + RMS_EPS)
    return (y * g.astype(jnp.float32)).astype(x.dtype)


def centred_shift(x):
    prev = jnp.pad(x[:, :-1], ((0, 0), (1, 0), (0, 0)))
    nxt = jnp.pad(x[:, 1:], ((0, 0), (0, 1), (0, 0)))
    return 0.5 * (prev + nxt)


def window_mean(z, window):
    lr = z.shape[2]
    cs = jnp.cumsum(z.astype(jnp.float32), axis=2)
    cs = jnp.pad(cs, ((0, 0), (0, 0), (1, 0), (0, 0)))
    t = jnp.arange(lr)
    lo = jnp.clip(t - window // 2, 0, lr)
    hi = jnp.clip(t + window - window // 2, 0, lr)
    total = jnp.take(cs, hi, axis=2) - jnp.take(cs, lo, axis=2)
    cnt = (hi - lo).astype(jnp.float32)[:, None]
    return (total / cnt).astype(z.dtype)


def pool_branch(z, rows, pool_w, pool_scale):
    b, l, _ = z.shape
    zr = z.reshape(b, rows, l // rows, N_POOL_GROUPS, POOL_GROUP)
    outs = []
    for g, win in enumerate(POOL_WINDOWS):
        zg = zr[..., g, :]
        mixed = window_mean(zg, win) - zg
        outs.append(jnp.einsum('brtc,cd->brtd', mixed, pool_w[g]))
    out = jnp.concatenate(outs, axis=-1).reshape(b, l, D_MODEL)
    return out * pool_scale


def wkv_scan(r, w, k, v, kk, a, s0, reverse):
    b, l, _ = r.shape

    def heads(t):
        return jnp.transpose(t.astype(jnp.float32).reshape(b, l, N_HEADS, HEAD_DIM), (1, 0, 2, 3))

    xs = (heads(r), heads(w), heads(k), heads(v), heads(kk), heads(a))

    def step(S, inp):
        r_t, w_t, k_t, v_t, kk_t, a_t = inp
        s_kk = jnp.einsum('bhij,bhj->bhi', S, kk_t)
        S = (S * w_t[:, :, None, :]
             - s_kk[..., :, None] * (kk_t * a_t)[..., None, :]
             + v_t[..., :, None] * k_t[..., None, :])
        y_t = jnp.einsum('bhij,bhj->bhi', S, r_t)
        return S, y_t

    s_final, ys = lax.scan(step, s0.astype(jnp.float32), xs, reverse=reverse)
    return jnp.transpose(ys, (1, 0, 2, 3)).reshape(b, l, D_RWKV), s_final


def rwkv_branch(h, rkv, s0_fwd, s0_bwd, p):
    b, l, _ = h.shape
    f32 = jnp.float32
    rkv = rkv + (centred_shift(rkv) - rkv) * p["mu_rkv"].reshape(-1)
    r, k, v = jnp.split(rkv, 3, axis=-1)
    hd = centred_shift(h) - h
    xw = h + hd * p["mu_wag"][0]
    xa = h + hd * p["mu_wag"][1]
    xg = h + hd * p["mu_wag"][2]
    g = jax.nn.sigmoid(xg @ p["gate_w1"]) @ p["gate_w2"]
    kk = (k * p["k_k"]).astype(f32).reshape(b, l, N_HEADS, HEAD_DIM)
    kk = (kk * lax.rsqrt(jnp.sum(kk * kk, axis=-1, keepdims=True) + NORM_EPS)).reshape(b, l, D_RWKV)
    s0 = (s0_fwd, s0_bwd)
    ys, states = [], []
    for d in range(N_DIR):
        pre_w = (p["w_dec0"][d] + jnp.tanh(xw @ p["w_dec1"][d]) @ p["w_dec2"][d]).astype(f32)
        decay = jnp.exp(-jnp.exp(-jax.nn.softplus(-pre_w) - 0.5))
        a = jax.nn.sigmoid((p["a0"][d] + (xa @ p["a1"][d]) @ p["a2"][d]).astype(f32))
        kd = k * (1.0 + (a - 1.0) * p["k_a"])
        y_d, s_d = wkv_scan(r, decay, kd, v, kk, a, s0[d], reverse=(d == 1))
        ys.append(y_d)
        states.append(s_d)
    y = (ys[0] + ys[1]).reshape(b, l, N_HEADS, HEAD_DIM)
    mu = jnp.mean(y, axis=-1, keepdims=True)
    var = jnp.mean(jnp.square(y - mu), axis=-1, keepdims=True)
    y = ((y - mu) * lax.rsqrt(var + LNX_EPS)).reshape(b, l, D_RWKV) * p["ln_x_w"] + p["ln_x_b"]
    bonus = jnp.sum((r * k * p["r_k"].reshape(-1)).astype(f32).reshape(b, l, N_HEADS, HEAD_DIM),
                    axis=-1, keepdims=True) * v.astype(f32).reshape(b, l, N_HEADS, HEAD_DIM)
    y = y + bonus.reshape(b, l, D_RWKV)
    return (y * g).astype(h.dtype), states[0], states[1]


def mixer(h, s0_fwd, s0_bwd, rows, p):
    z = h @ p["w_in"]
    z_pool = z[..., :D_POOL]
    rkv = z[..., D_POOL:D_POOL + 3 * D_RWKV]
    gate_logits = z[..., D_POOL + 3 * D_RWKV:]
    a_out = pool_branch(z_pool, rows, p["pool_w"], p["pool_scale"])
    b_out, s_fwd, s_bwd = rwkv_branch(h, rkv, s0_fwd, s0_bwd, p)
    gate_a, gate_b = jnp.split(jax.nn.sigmoid(gate_logits), 2, axis=-1)
    out = (gate_a * a_out + gate_b * b_out) @ p["w_out"]
    return out, s_fwd, s_bwd


def layer(x, cvec, s0_fwd, s0_bwd, rows, p):
    mod = jax.nn.silu(cvec) @ p["w_ada"] + p["b_ada"]
    sh1, sc1, ga1, sh2, sc2, ga2 = jnp.split(mod[:, None, :], N_MOD, axis=-1)
    h = rmsnorm(x, p["g_norm1"]) * (1.0 + sc1) + sh1
    mix, s_fwd, s_bwd = mixer(h, s0_fwd, s0_bwd, rows, p)
    x = x + ga1 * mix
    h2 = rmsnorm(x, p["g_norm2"]) * (1.0 + sc2) + sh2
    ff = jnp.square(jax.nn.relu(h2 @ p["w_ff1"])) @ p["w_ff2"]
    x = x + ga2 * ff
    return x, s_fwd, s_bwd


def setup_inputs(seed: int = 0) -> dict:
    key = jax.random.key(seed)
    ks = iter(jax.random.split(key, 40))
    L = DEPTH
    D = D_MODEL

    def nrm(shape, scale):
        return scale * jax.random.normal(next(ks), shape, jnp.float32)

    def uni(shape, lo, hi):
        return jax.random.uniform(next(ks), shape, jnp.float32, lo, hi)

    return {
        "x_prompt": nrm((BATCH, SEQ, D), 1.0),
        "x_sample": nrm((DEC_BATCH, DEC_SEQ, D), 1.0),
        "state_rwkv": nrm((DEC_BATCH, DEPTH, N_DIR, N_HEADS, HEAD_DIM, HEAD_DIM), 0.5),
        "c": nrm((DEC_BATCH, D), 1.0),
        "c_ctx": nrm((D,), 1.0),
        "w_ada": nrm((L, D, N_MOD * D), 0.5 * D ** -0.5),
        "b_ada": nrm((L, N_MOD * D), 0.02),
        "g_norm1": 1.0 + nrm((L, D), 0.05),
        "g_norm2": 1.0 + nrm((L, D), 0.05),
        "w_in": nrm((L, D, D_IN), D ** -0.5),
        "mu_rkv": uni((L, 3, D_RWKV), 0.0, 1.0),
        "mu_wag": uni((L, 3, D), 0.0, 1.0),
        "w_dec0": uni((L, N_DIR, D_RWKV), -4.0, -1.0),
        "w_dec1": nrm((L, N_DIR, D, DECAY_LORA), D ** -0.5),
        "w_dec2": nrm((L, N_DIR, DECAY_LORA, D_RWKV), 0.5 * DECAY_LORA ** -0.5),
        "a0": nrm((L, N_DIR, D_RWKV), 0.5),
        "a1": nrm((L, N_DIR, D, AAA_LORA), D ** -0.5),
        "a2": nrm((L, N_DIR, AAA_LORA, D_RWKV), 0.5 * AAA_LORA ** -0.5),
        "gate_w1": nrm((L, D, GATE_LORA), D ** -0.5),
        "gate_w2": nrm((L, GATE_LORA, D_RWKV), GATE_LORA ** -0.5),
        "k_k": 0.85 + nrm((L, D_RWKV), 0.05),
        "k_a": 1.0 + nrm((L, D_RWKV), 0.05),
        "r_k": nrm((L, N_HEADS, HEAD_DIM), 0.1),
        "ln_x_w": 1.0 + nrm((L, D_RWKV), 0.05),
        "ln_x_b": nrm((L, D_RWKV), 0.02),
        "pool_w": nrm((L, N_POOL_GROUPS, POOL_GROUP, POOL_OUT_GROUP), POOL_GROUP ** -0.5),
        "pool_scale": 0.5 + nrm((L, D), 0.05),
        "w_out": nrm((L, D, D), D ** -0.5),
        "w_ff1": nrm((L, D, D_FF), D ** -0.5),
        "w_ff2": nrm((L, D_FF, D), D_FF ** -0.5),
        "g_final": 1.0 + nrm((D,), 0.05),
    }


def reference(x_prompt, x_sample, state_rwkv, c, c_ctx, w_ada, b_ada, g_norm1, g_norm2, w_in,
              mu_rkv, mu_wag, w_dec0, w_dec1, w_dec2, a0, a1, a2, gate_w1, gate_w2, k_k, k_a,
              r_k, ln_x_w, ln_x_b, pool_w, pool_scale, w_out, w_ff1, w_ff2, g_final):
    weights = (w_ada, b_ada, g_norm1, g_norm2, w_in, mu_rkv, mu_wag, w_dec0, w_dec1, w_dec2,
               a0, a1, a2, gate_w1, gate_w2, k_k, k_a, r_k, ln_x_w, ln_x_b, pool_w, pool_scale,
               w_out, w_ff1, w_ff2)
    rows = x_sample.shape[1] // GRID_W
    zero_state = jnp.zeros((x_prompt.shape[0], N_HEADS, HEAD_DIM, HEAD_DIM), jnp.float32)
    cvec_ctx = c_ctx[None, :]
    ctx = x_prompt
    lat = x_sample
    new_states = []
    for li in range(DEPTH):
        p = dict(zip(LAYER_KEYS, (w[li] for w in weights)))
        ctx, sf, sb = layer(ctx, cvec_ctx, zero_state, zero_state, 1, p)
        new_states.append(jnp.stack([sf, sb], axis=1))
        lat, _, _ = layer(lat, c, state_rwkv[:, li, 0], state_rwkv[:, li, 1], rows, p)
    y_prompt = rmsnorm(ctx, g_final)
    y_sample = rmsnorm(lat, g_final)
    state_rwkv_new = jnp.stack(new_states, axis=1)
    return (y_prompt, y_sample, state_rwkv_new)
```

```python
import functools

import numpy as np
import jax
import jax.numpy as jnp
from jax import lax
from jax.experimental import pallas as pl
from jax.experimental.pallas import tpu as pltpu

F32 = jnp.float32
BF16 = jnp.bfloat16

N_HEADS = 16
HEAD_DIM = 64
PAIR = 2 * HEAD_DIM
POOL_WINDOWS = (2, 4, 8, 16)
POOL_GROUP = 128
GRID_W = 64
N_MOD = 6
RMS_EPS = 1e-6
LNX_EPS = 64e-5
NORM_EPS = 1e-12

CHUNK = 64
TOKEN_TILE = 256
SEG_TILE = 256
SUBLANES = 8
VMEM_LIMIT = 56 * 1024 * 1024


def _dot(a, b):
    return jnp.dot(a, b, preferred_element_type=F32)


def _dot_nt(a, b):
    return lax.dot_general(a, b, (((1,), (1,)), ((), ())), preferred_element_type=F32)


def _dot_tn(a, b):
    return lax.dot_general(a, b, (((0,), (0,)), ((), ())), preferred_element_type=F32)


def _dot_f32(a, b):
    return jnp.dot(a, b, precision=lax.Precision.HIGHEST, preferred_element_type=F32)


def _split3(x):
    h1 = x.astype(BF16)
    r1 = x - h1.astype(F32)
    h2 = r1.astype(BF16)
    h3 = (r1 - h2.astype(F32)).astype(BF16)
    return h1, h2, h3


def _dot_exact_rhs(x, m):
    h1, h2, h3 = _split3(x)
    return _dot(h1, m) + _dot(h2, m) + _dot(h3, m)


def _dot_exact_lhs(m, x):
    h1, h2, h3 = _split3(x)
    return _dot(m, h1) + _dot(m, h2) + _dot(m, h3)


def _seg_sum(x, seg_ones):
    d = x.shape[-1]
    parts = [_dot_exact_rhs(x[:, q:q + SEG_TILE], seg_ones) for q in range(0, d, SEG_TILE)]
    return jnp.concatenate(parts, axis=-1)


def _sigmoid(x):
    return 1.0 / (1.0 + jnp.exp(-x))


def _modulated_norm(x, g, scale, shift):
    y = x * lax.rsqrt(jnp.mean(x * x, axis=-1, keepdims=True) + RMS_EPS)
    return (y * g) * (1.0 + scale) + shift


def _shift_rows(v, first_row, last_row):
    n = v.shape[0]
    row = lax.broadcasted_iota(jnp.int32, v.shape, 0)
    up = jnp.where(row == 0, first_row, pltpu.roll(v, 1, 0))
    dn = jnp.where(row == n - 1, last_row, pltpu.roll(v, n - 1, 0))
    return up, dn


def _tile_h(x_ref, xp_ref, xn_ref, g, scale, shift):
    j = pl.program_id(1)
    nj = pl.num_programs(1)
    h = _modulated_norm(x_ref[...], g, scale, shift)
    halo = jnp.concatenate([xp_ref[SUBLANES - 1:SUBLANES, :], xn_ref[0:1, :]], axis=0)
    hh = _modulated_norm(halo, g, scale, shift)
    h_prev = jnp.where(j > 0, hh[0:1, :], 0.0)
    h_next = jnp.where(j < nj - 1, hh[1:2, :], 0.0)
    return h, h_prev, h_next


def _ada_kernel(cv_ref, w_ref, b_ref, o_ref):
    cv = cv_ref[...]
    o_ref[...] = _dot_f32(cv * _sigmoid(cv), w_ref[...]) + b_ref[...]


def _ada_call(cv, w_ada, b_ada):
    rows, d = cv.shape
    n = w_ada.shape[1]
    bn = n // 4
    return pl.pallas_call(
        _ada_kernel,
        out_shape=jax.ShapeDtypeStruct((rows, n), F32),
        grid=(n // bn,),
        in_specs=[pl.BlockSpec((rows, d), lambda i: (0, 0)),
                  pl.BlockSpec((d, bn), lambda i: (0, i)),
                  pl.BlockSpec((1, bn), lambda i: (0, i))],
        out_specs=pl.BlockSpec((rows, bn), lambda i: (0, i)),
        compiler_params=pltpu.CompilerParams(dimension_semantics=("arbitrary",),
                                             vmem_limit_bytes=VMEM_LIMIT),
        name="ada_mod",
    )(cv, w_ada, b_ada)


def _pre_kernel(x_ref, xp_ref, xn_ref, mod_ref, g1_ref, w_rkv_ref, mu_rkv_ref, mu_wag_ref,
                wd1_ref, wd2_ref, wd0_ref, a1_ref, a2_ref, a0_ref, kk_w_ref, ka_ref, rk_ref,
                seg_ref,
                lw0_ref, lw1_ref, r_ref, kd0_ref, kd1_ref, v_ref, kk_ref, b0_ref, b1_ref,
                bonus_ref):
    d = x_ref.shape[-1]
    mod = mod_ref[...]
    shift, scale = mod[:, 0:d], mod[:, d:2 * d]
    h, h_prev, h_next = _tile_h(x_ref, xp_ref, xn_ref, g1_ref[...], scale, shift)
    h_up, h_dn = _shift_rows(h, h_prev, h_next)
    hd = 0.5 * (h_up + h_dn) - h

    w_rkv = w_rkv_ref[...]
    z = _dot(h.astype(BF16), w_rkv)
    halo = jnp.concatenate([h_prev, h_next, jnp.zeros((SUBLANES - 2, d), F32)], axis=0)
    zh = _dot(halo.astype(BF16), w_rkv)
    z_up, z_dn = _shift_rows(z, zh[0:1, :], zh[1:2, :])
    z = z + (0.5 * (z_up + z_dn) - z) * mu_rkv_ref[...]
    r, k, v = z[:, 0:d], z[:, d:2 * d], z[:, 2 * d:3 * d]

    mu = mu_wag_ref[...]
    xw = h + hd * mu[0:1, :]
    xa = h + hd * mu[1:2, :]

    tw = jnp.tanh(_dot(xw.astype(BF16), wd1_ref[...]))
    pre_w = wd0_ref[...] + _dot(tw.astype(BF16), wd2_ref[...])
    lw = -_sigmoid(pre_w) * float(np.exp(-0.5))
    ta = _dot(xa.astype(BF16), a1_ref[...])
    a = _sigmoid(a0_ref[...] + _dot(ta.astype(BF16), a2_ref[...]))

    seg = seg_ref[...]
    kkr = k * kk_w_ref[...]
    kk = kkr * lax.rsqrt(_seg_sum(kkr * kkr, seg) + NORM_EPS)
    ka = ka_ref[...]

    lw0_ref[...] = lw[:, 0:d]
    lw1_ref[...] = lw[:, d:2 * d]
    r_ref[...] = r
    v_ref[...] = v
    kk_ref[...] = kk
    a_f, a_b = a[:, 0:d], a[:, d:2 * d]
    kd0_ref[...] = k * (1.0 + (a_f - 1.0) * ka)
    kd1_ref[...] = k * (1.0 + (a_b - 1.0) * ka)
    b0_ref[...] = kk * a_f
    b1_ref[...] = kk * a_b
    bonus_ref[...] = _seg_sum(r * k * rk_ref[...], seg) * v


def _const_spec(shape):
    nd = len(shape)
    return pl.BlockSpec(shape, lambda b, j: (0,) * nd, pipeline_mode=pl.Buffered(1))


def _token_specs(tm, seq, d):
    tiles8 = tm // SUBLANES
    last8 = seq // SUBLANES - 1
    x_spec = pl.BlockSpec((None, tm, d), lambda b, j: (b, j, 0))
    xp_spec = pl.BlockSpec((None, SUBLANES, d), lambda b, j: (b, jnp.maximum(j * tiles8 - 1, 0), 0))
    xn_spec = pl.BlockSpec((None, SUBLANES, d), lambda b, j: (b, jnp.minimum((j + 1) * tiles8, last8), 0))
    return x_spec, xp_spec, xn_spec


def _pre_call(x, mod, p, consts):
    bsz, seq, d = x.shape
    tm = TOKEN_TILE
    x_spec, xp_spec, xn_spec = _token_specs(tm, seq, d)
    mod_spec = pl.BlockSpec((None, 1, mod.shape[-1]), lambda b, j: (b % mod.shape[0], 0, 0))
    weights = (p["g1"], p["w_rkv"], p["mu_rkv"], p["mu_wag"], p["wd1"], p["wd2"], p["wd0"],
               p["a1"], p["a2"], p["a0"], p["k_k"], p["k_a"], p["r_k"], consts["seg"])
    out = jax.ShapeDtypeStruct((bsz, seq, d), F32)
    return pl.pallas_call(
        _pre_kernel,
        out_shape=(out,) * 10,
        grid=(bsz, seq // tm),
        in_specs=[x_spec, xp_spec, xn_spec, mod_spec] + [_const_spec(w.shape) for w in weights],
        out_specs=(x_spec,) * 10,
        compiler_params=pltpu.CompilerParams(dimension_semantics=("parallel", "parallel"),
                                             vmem_limit_bytes=VMEM_LIMIT),
        name="pre_scan",
    )(x, x, x, mod, *weights)


def _stack(x, lane_head):
    return jnp.concatenate([jnp.where(lane_head == 0, x, 0.0), jnp.where(lane_head == 1, x, 0.0)], axis=0)


def _chunk_step(lw, r, kd, v, kk, b, s_bd, reverse, cst):
    c = lw.shape[0]
    tri, strict, incl, eye, lane_head = cst["tri"][reverse], cst["strict"][reverse], cst["incl"][reverse], cst["eye"], cst["lane_head"]
    cum = _dot_exact_lhs(tri, lw)
    last = 0 if reverse else c - 1
    tot = cum[last:last + 1, :]
    e_in = jnp.exp(cum)
    e_ex = jnp.exp(cum - lw)
    e_ng = jnp.exp(-cum)
    e_end = jnp.exp(tot - cum)
    a_st = _stack(kk * e_ex, lane_head)
    r_st = _stack(r * e_in, lane_head)
    b_st = _stack(b * e_ng, lane_head)
    k_st = _stack(kd * e_ng, lane_head)
    bh_st = _stack(-(b * e_end), lane_head)
    kh_st = _stack(kd * e_end, lane_head)
    v_st = _stack(v, lane_head)

    ar = jnp.concatenate([a_st, r_st], axis=0).astype(BF16)
    sb = _dot_nt(ar, b_st.astype(BF16))
    sk = _dot_nt(ar, k_st.astype(BF16))
    n = 2 * c
    l_m = jnp.where(strict, -sb[:n], 0.0)
    m_ak = jnp.where(strict, sk[:n], 0.0)
    m_rb = jnp.where(incl, -sb[n:], 0.0)
    m_rk = jnp.where(incl, sk[n:], 0.0)

    x_m = jnp.where(eye, 1.0, 0.0) + l_m
    p_m = _dot_f32(l_m, l_m)
    n_sq = int(np.log2(c)) - 1
    for i in range(n_sq):
        if i < n_sq - 1:
            zz = _dot_f32(jnp.concatenate([x_m, p_m], axis=0), p_m)
            x_m = x_m + zz[:n]
            p_m = zz[n:]
        else:
            x_m = x_m + _dot_f32(x_m, p_m)

    v_b = v_st.astype(BF16)
    mv = _dot(m_ak.astype(BF16), v_b)
    w_m = _dot_f32(x_m, jnp.concatenate([a_st, mv], axis=1))
    w_b = w_m.astype(BF16)
    ry = _dot(m_rb.astype(BF16), w_b)
    r_p = r_st + ry[:, :n]
    y_v = ry[:, n:] + _dot(m_rk.astype(BF16), v_b)
    gh = _dot_tn(bh_st.astype(BF16), w_b)
    g_m = jnp.where(eye, jnp.exp(tot), 0.0) + gh[:, :n]
    h_m = gh[:, n:] + _dot_tn(kh_st.astype(BF16), v_b)

    y_st = _dot_f32(r_p, s_bd) + y_v
    y = y_st[:c] + y_st[c:]
    s_new = _dot_f32(g_m, s_bd) + h_m
    return y, s_new


def _scan_consts(c):
    n = 2 * c
    row = lax.broadcasted_iota(jnp.int32, (n, n), 0)
    col = lax.broadcasted_iota(jnp.int32, (n, n), 1)
    same = (row // c) == (col // c)
    t, s = row % c, col % c
    tr = lax.broadcasted_iota(jnp.int32, (c, c), 0)
    tc = lax.broadcasted_iota(jnp.int32, (c, c), 1)
    lane = lax.broadcasted_iota(jnp.int32, (c, PAIR), 1)
    return {
        "tri": (jnp.where(tc <= tr, 1.0, 0.0).astype(BF16), jnp.where(tc >= tr, 1.0, 0.0).astype(BF16)),
        "strict": (same & (s < t), same & (s > t)),
        "incl": (same & (s <= t), same & (s >= t)),
        "eye": row == col,
        "lane_head": lane // HEAD_DIM,
    }


def _scan_kernel(*refs, has_s0, want_final):
    lw0_ref, lw1_ref, r_ref, kd0_ref, kd1_ref, v_ref, kk_ref, b0_ref, b1_ref = refs[:9]
    pos = 9
    s0_ref = None
    if has_s0:
        s0_ref = refs[pos]
        pos += 1
    y_ref = refs[pos]
    pos += 1
    sf_ref = None
    if want_final:
        sf_ref = refs[pos]
        pos += 1
    yb_ref, st_ref = refs[pos], refs[pos + 1]

    c = CHUNK
    seq = r_ref.shape[0]
    nc = seq // c
    cst = _scan_consts(c)
    zero = jnp.zeros((HEAD_DIM, HEAD_DIM), F32)

    def init_state(dr):
        if not has_s0:
            return jnp.zeros((PAIR, PAIR), F32)
        s_a = s0_ref[dr, 0].T
        s_b = s0_ref[dr, 1].T
        return jnp.concatenate([jnp.concatenate([s_a, zero], axis=1),
                                jnp.concatenate([zero, s_b], axis=1)], axis=0)

    st_ref[0] = init_state(0)
    st_ref[1] = init_state(1)

    def body(i, carry):
        of = pl.multiple_of(i * c, c)
        ob = pl.multiple_of((nc - 1 - i) * c, c)
        sl_f = pl.ds(of, c)
        sl_b = pl.ds(ob, c)
        y_f, s_f = _chunk_step(lw0_ref[sl_f, :], r_ref[sl_f, :], kd0_ref[sl_f, :], v_ref[sl_f, :],
                               kk_ref[sl_f, :], b0_ref[sl_f, :], st_ref[0], 0, cst)
        y_b, s_b = _chunk_step(lw1_ref[sl_b, :], r_ref[sl_b, :], kd1_ref[sl_b, :], v_ref[sl_b, :],
                               kk_ref[sl_b, :], b1_ref[sl_b, :], st_ref[1], 1, cst)
        y_ref[sl_f, :] = y_f
        yb_ref[sl_b, :] = y_b
        st_ref[0] = s_f
        st_ref[1] = s_b
        return carry

    lax.fori_loop(0, nc, body, 0)
    s_f, s_b = st_ref[0], st_ref[1]
    y_ref[...] = y_ref[...] + yb_ref[...]
    if want_final:
        for dr, s_bd in ((0, s_f), (1, s_b)):
            sf_ref[dr, 0] = s_bd[:HEAD_DIM, :HEAD_DIM].T
            sf_ref[dr, 1] = s_bd[HEAD_DIM:, HEAD_DIM:].T


def _scan_call(ops, s0, want_final):
    bsz, seq, d = ops[0].shape
    n_pairs = d // PAIR
    col_spec = pl.BlockSpec((None, seq, PAIR), lambda b, q: (b, 0, q))
    st_spec = pl.BlockSpec((None, 2, 2, HEAD_DIM, HEAD_DIM), lambda b, q: (b, 0, q, 0, 0))
    in_specs = [col_spec] * 9
    args = list(ops)
    has_s0 = s0 is not None
    if has_s0:
        in_specs.append(st_spec)
        args.append(s0)
    out_shape = [jax.ShapeDtypeStruct((bsz, seq, d), F32)]
    out_specs = [col_spec]
    if want_final:
        out_shape.append(jax.ShapeDtypeStruct((bsz, 2, N_HEADS, HEAD_DIM, HEAD_DIM), F32))
        out_specs.append(st_spec)
    res = pl.pallas_call(
        functools.partial(_scan_kernel, has_s0=has_s0, want_final=want_final),
        out_shape=tuple(out_shape),
        grid=(bsz, n_pairs),
        in_specs=in_specs,
        out_specs=tuple(out_specs),
        scratch_shapes=[pltpu.VMEM((seq, PAIR), F32), pltpu.VMEM((2, PAIR, PAIR), F32)],
        compiler_params=pltpu.CompilerParams(dimension_semantics=("parallel", "parallel"),
                                             vmem_limit_bytes=VMEM_LIMIT),
        name="wkv_scan",
    )(*args)
    return res if want_final else (res[0], None)


def _post_kernel(x_ref, xp_ref, xn_ref, mod_ref, y_ref, bonus_ref, g1_ref, g2_ref, gf_ref,
                 w_pg_ref, mu_wag_ref, gw1_ref, gw2_ref, lnw_ref, lnb_ref, pool_w_ref, pool_scale_ref,
                 band_ref, icnt_ref, seg_ref, w_out_ref, w_ff1_ref, w_ff2_ref, o_ref, *, final_norm):
    d = x_ref.shape[-1]
    d_pool = len(POOL_WINDOWS) * POOL_GROUP
    mod = mod_ref[...]
    sh1, sc1, ga1, sh2, sc2, ga2 = (mod[:, i * d:(i + 1) * d] for i in range(N_MOD))
    x = x_ref[...]
    h, h_prev, h_next = _tile_h(x_ref, xp_ref, xn_ref, g1_ref[...], sc1, sh1)
    h_up, h_dn = _shift_rows(h, h_prev, h_next)
    xg = h + (0.5 * (h_up + h_dn) - h) * mu_wag_ref[2:3, :]

    z = _dot(h.astype(BF16), w_pg_ref[...])
    outs = []
    for g in range(len(POOL_WINDOWS)):
        zg = z[:, g * POOL_GROUP:(g + 1) * POOL_GROUP]
        zh = zg.astype(BF16)
        zl = (zg - zh.astype(F32)).astype(BF16)
        band = band_ref[g]
        mixed = (_dot(band, zh) + _dot(band, zl)) * icnt_ref[g] - zg
        outs.append(_dot(mixed.astype(BF16), pool_w_ref[g]))
    a_out = jnp.concatenate(outs, axis=-1) * pool_scale_ref[...]
    gate_a = _sigmoid(z[:, d_pool:d_pool + d])
    gate_b = _sigmoid(z[:, d_pool + d:d_pool + 2 * d])
    gl = _sigmoid(_dot(xg.astype(BF16), gw1_ref[...]))
    g_out = _dot(gl.astype(BF16), gw2_ref[...])

    seg = seg_ref[...]
    y = y_ref[...]
    yc = y - _seg_sum(y, seg) * (1.0 / HEAD_DIM)
    var = _seg_sum(yc * yc, seg) * (1.0 / HEAD_DIM)
    yn = yc * lax.rsqrt(var + LNX_EPS) * lnw_ref[...] + lnb_ref[...]
    b_out = (yn + bonus_ref[...]) * g_out
    mix = _dot((gate_a * a_out + gate_b * b_out).astype(BF16), w_out_ref[...])
    x = x + ga1 * mix

    h2 = _modulated_norm(x, g2_ref[...], sc2, sh2)
    u = jnp.maximum(_dot(h2.astype(BF16), w_ff1_ref[...]), 0.0)
    ff = _dot((u * u).astype(BF16), w_ff2_ref[...])
    x = x + ga2 * ff
    if final_norm:
        x = x * lax.rsqrt(jnp.mean(x * x, axis=-1, keepdims=True) + RMS_EPS) * gf_ref[...]
    o_ref[...] = x


def _post_call(x, mod, y, bonus, p, consts, g_final, final_norm):
    bsz, seq, d = x.shape
    tm = TOKEN_TILE
    x_spec, xp_spec, xn_spec = _token_specs(tm, seq, d)
    mod_spec = pl.BlockSpec((None, 1, mod.shape[-1]), lambda b, j: (b % mod.shape[0], 0, 0))
    weights = (p["g1"], p["g2"], g_final, p["w_pg"], p["mu_wag"], p["gw1"], p["gw2"], p["ln_x_w"],
               p["ln_x_b"], p["pool_w"], p["pool_scale"], consts["band"], consts["icnt"], consts["seg"],
               p["w_out"], p["w_ff1"], p["w_ff2"])
    return pl.pallas_call(
        functools.partial(_post_kernel, final_norm=final_norm),
        out_shape=jax.ShapeDtypeStruct((bsz, seq, d), F32),
        grid=(bsz, seq // tm),
        in_specs=[x_spec, xp_spec, xn_spec, mod_spec, x_spec, x_spec] + [_const_spec(w.shape) for w in weights],
        out_specs=x_spec,
        compiler_params=pltpu.CompilerParams(dimension_semantics=("parallel", "parallel"),
                                             vmem_limit_bytes=VMEM_LIMIT),
        name="post_scan",
    )(x, x, x, mod, y, bonus, *weights)


def _pool_consts(tm, row_len):
    t = np.arange(tm)
    pos, row = t % row_len, t // row_len
    bands, icnts = [], []
    for win in POOL_WINDOWS:
        lo = np.clip(pos - win // 2, 0, row_len)
        hi = np.clip(pos + win - win // 2, 0, row_len)
        inside = (pos[None, :] >= lo[:, None]) & (pos[None, :] < hi[:, None]) & (row[None, :] == row[:, None])
        bands.append(inside.astype(np.float32))
        icnts.append(np.broadcast_to((1.0 / (hi - lo))[:, None], (tm, POOL_GROUP)).astype(np.float32))
    return jnp.asarray(np.stack(bands), BF16), jnp.asarray(np.stack(icnts), F32)


def _block_diag2(w):
    z = jnp.zeros_like(w[0])
    return jnp.concatenate([jnp.concatenate([w[0], z], axis=1), jnp.concatenate([z, w[1]], axis=1)], axis=0)


def _layer_params(li, w):
    d = w["w_in"].shape[1]
    d_pool = len(POOL_WINDOWS) * POOL_GROUP
    w_in = w["w_in"][li].astype(BF16)
    row = lambda a: a.reshape(1, -1)
    return {
        "g1": row(w["g_norm1"][li]), "g2": row(w["g_norm2"][li]),
        "w_rkv": w_in[:, d_pool:d_pool + 3 * d],
        "w_pg": jnp.concatenate([w_in[:, :d_pool], w_in[:, d_pool + 3 * d:]], axis=1),
        "mu_rkv": row(w["mu_rkv"][li]), "mu_wag": w["mu_wag"][li],
        "wd1": jnp.concatenate([w["w_dec1"][li, 0], w["w_dec1"][li, 1]], axis=1).astype(BF16),
        "wd2": _block_diag2(w["w_dec2"][li]).astype(BF16), "wd0": row(w["w_dec0"][li]),
        "a1": jnp.concatenate([w["a1"][li, 0], w["a1"][li, 1]], axis=1).astype(BF16),
        "a2": _block_diag2(w["a2"][li]).astype(BF16), "a0": row(w["a0"][li]),
        "gw1": w["gate_w1"][li].astype(BF16), "gw2": w["gate_w2"][li].astype(BF16),
        "k_k": row(w["k_k"][li]), "k_a": row(w["k_a"][li]), "r_k": row(w["r_k"][li]),
        "ln_x_w": row(w["ln_x_w"][li]), "ln_x_b": row(w["ln_x_b"][li]),
        "pool_w": w["pool_w"][li].astype(BF16), "pool_scale": row(w["pool_scale"][li]),
        "w_out": w["w_out"][li].astype(BF16), "w_ff1": w["w_ff1"][li].astype(BF16),
        "w_ff2": w["w_ff2"][li].astype(BF16),
    }


def _layer(x, mod, s0, want_final, p, consts, g_final, final_norm):
    ops = _pre_call(x, mod, p, consts)
    y, s_fin = _scan_call(ops[:9], s0, want_final)
    return _post_call(x, mod, y, ops[9], p, consts, g_final, final_norm), s_fin


def kernel(x_prompt, x_sample, state_rwkv, c, c_ctx, w_ada, b_ada, g_norm1, g_norm2, w_in, mu_rkv, mu_wag, w_dec0, w_dec1, w_dec2, a0, a1, a2, gate_w1, gate_w2, k_k, k_a, r_k, ln_x_w, ln_x_b, pool_w, pool_scale, w_out, w_ff1, w_ff2, g_final):
    w = dict(w_ada=w_ada, b_ada=b_ada, g_norm1=g_norm1, g_norm2=g_norm2, w_in=w_in, mu_rkv=mu_rkv,
             mu_wag=mu_wag, w_dec0=w_dec0, w_dec1=w_dec1, w_dec2=w_dec2, a0=a0, a1=a1, a2=a2,
             gate_w1=gate_w1, gate_w2=gate_w2, k_k=k_k, k_a=k_a, r_k=r_k, ln_x_w=ln_x_w, ln_x_b=ln_x_b,
             pool_w=pool_w, pool_scale=pool_scale, w_out=w_out, w_ff1=w_ff1, w_ff2=w_ff2)
    depth = w_in.shape[0]
    n_lat, lat_len, d = x_sample.shape
    ctx_len = x_prompt.shape[1]
    assert ctx_len % TOKEN_TILE == 0 and lat_len % TOKEN_TILE == 0 and TOKEN_TILE % GRID_W == 0
    assert ctx_len == TOKEN_TILE, "context pooling runs over the whole sequence inside one token tile"

    seg = jnp.asarray(np.kron(np.eye(SEG_TILE // HEAD_DIM), np.ones((HEAD_DIM, HEAD_DIM))), BF16)
    band_c, icnt_c = _pool_consts(TOKEN_TILE, ctx_len)
    band_l, icnt_l = _pool_consts(TOKEN_TILE, GRID_W)
    consts_ctx = {"seg": seg, "band": band_c, "icnt": icnt_c}
    consts_lat = {"seg": seg, "band": band_l, "icnt": icnt_l}

    n_cv = 1 + n_lat
    cv = jnp.concatenate([c_ctx[None, :], c, jnp.zeros((-n_cv % SUBLANES, d), F32)], axis=0)
    g_fin = g_final.reshape(1, d)
    ctx, lat = x_prompt, x_sample
    new_states = []
    for li in range(depth):
        p = _layer_params(li, w)
        mod = _ada_call(cv, w_ada[li], b_ada[li].reshape(1, -1))
        mod_ctx = mod[0:1].reshape(1, 1, -1)
        mod_lat = mod[1:n_cv].reshape(n_lat, 1, -1)
        last = li == depth - 1
        ctx, s_fin = _layer(ctx, mod_ctx, None, True, p, consts_ctx, g_fin, last)
        new_states.append(s_fin)
        lat, _ = _layer(lat, mod_lat, state_rwkv[:, li], False, p, consts_lat, g_fin, last)
    return ctx, lat, jnp.stack(new_states, axis=1)
```

```python
import functools

import numpy as np
import jax
import jax.numpy as jnp
from jax import lax
from jax.experimental import pallas as pl
from jax.experimental.pallas import tpu as pltpu

F32 = jnp.float32
BF16 = jnp.bfloat16

N_HEADS = 16
HEAD_DIM = 64
PAIR = 2 * HEAD_DIM
POOL_WINDOWS = (2, 4, 8, 16)
POOL_GROUP = 128
GRID_W = 64
N_MOD = 6
RMS_EPS = 1e-6
LNX_EPS = 64e-5
NORM_EPS = 1e-12

CHUNK = 64
CHUNKS_PER_STEP = 2
TOKEN_TILE = 256
SEG_TILE = 256
SUBLANES = 8
VMEM_LIMIT = 56 * 1024 * 1024


def _dot(a, b):
    return jnp.dot(a, b, preferred_element_type=F32)


def _dot_nt(a, b):
    return lax.dot_general(a, b, (((1,), (1,)), ((), ())), preferred_element_type=F32)


def _dot_tn(a, b):
    return lax.dot_general(a, b, (((0,), (0,)), ((), ())), preferred_element_type=F32)


def _dot_f32(a, b):
    return jnp.dot(a, b, precision=lax.Precision.HIGHEST, preferred_element_type=F32)


def _split3(x):
    h1 = x.astype(BF16)
    r1 = x - h1.astype(F32)
    h2 = r1.astype(BF16)
    h3 = (r1 - h2.astype(F32)).astype(BF16)
    return h1, h2, h3


def _dot_exact_rhs(x, m):
    h1, h2, h3 = _split3(x)
    return _dot(h1, m) + _dot(h2, m) + _dot(h3, m)


def _dot_exact_lhs(m, x):
    h1, h2, h3 = _split3(x)
    return _dot(m, h1) + _dot(m, h2) + _dot(m, h3)


def _seg_sum(x, seg_ones):
    d = x.shape[-1]
    parts = [_dot_exact_rhs(x[:, q:q + SEG_TILE], seg_ones) for q in range(0, d, SEG_TILE)]
    return jnp.concatenate(parts, axis=-1)


def _sigmoid(x):
    return 1.0 / (1.0 + jnp.exp(-x))


def _modulated_norm(x, g, scale, shift):
    y = x * lax.rsqrt(jnp.mean(x * x, axis=-1, keepdims=True) + RMS_EPS)
    return (y * g) * (1.0 + scale) + shift


def _shift_rows(v, first_row, last_row):
    n = v.shape[0]
    row = lax.broadcasted_iota(jnp.int32, v.shape, 0)
    up = jnp.where(row == 0, first_row, pltpu.roll(v, 1, 0))
    dn = jnp.where(row == n - 1, last_row, pltpu.roll(v, n - 1, 0))
    return up, dn


def _tile_h(x_ref, xp_ref, xn_ref, g, scale, shift):
    j = pl.program_id(1)
    nj = pl.num_programs(1)
    h = _modulated_norm(x_ref[...], g, scale, shift)
    halo = jnp.concatenate([xp_ref[SUBLANES - 1:SUBLANES, :], xn_ref[0:1, :]], axis=0)
    hh = _modulated_norm(halo, g, scale, shift)
    h_prev = jnp.where(j > 0, hh[0:1, :], 0.0)
    h_next = jnp.where(j < nj - 1, hh[1:2, :], 0.0)
    return h, h_prev, h_next


def _ada_kernel(cv_ref, w_ref, b_ref, o_ref):
    cv = cv_ref[...]
    o_ref[...] = _dot_f32(cv * _sigmoid(cv), w_ref[...]) + b_ref[...]


def _ada_call(cv, w_ada, b_ada):
    rows, d = cv.shape
    n = w_ada.shape[1]
    bn = n // 4
    return pl.pallas_call(
        _ada_kernel,
        out_shape=jax.ShapeDtypeStruct((rows, n), F32),
        grid=(n // bn,),
        in_specs=[pl.BlockSpec((rows, d), lambda i: (0, 0)),
                  pl.BlockSpec((d, bn), lambda i: (0, i)),
                  pl.BlockSpec((1, bn), lambda i: (0, i))],
        out_specs=pl.BlockSpec((rows, bn), lambda i: (0, i)),
        compiler_params=pltpu.CompilerParams(dimension_semantics=("arbitrary",),
                                             vmem_limit_bytes=VMEM_LIMIT),
        name="ada_mod",
    )(cv, w_ada, b_ada)


def _pre_kernel(x_ref, xp_ref, xn_ref, mod_ref, g1_ref, w_rkv_ref, mu_rkv_ref, mu_wag_ref,
                wd1_ref, wd2_ref, wd0_ref, a1_ref, a2_ref, a0_ref, kk_w_ref, ka_ref, rk_ref,
                seg_ref,
                lw0_ref, lw1_ref, r_ref, kd0_ref, kd1_ref, v_ref, kk_ref, b0_ref, b1_ref,
                bonus_ref):
    d = x_ref.shape[-1]
    mod = mod_ref[...]
    shift, scale = mod[:, 0:d], mod[:, d:2 * d]
    h, h_prev, h_next = _tile_h(x_ref, xp_ref, xn_ref, g1_ref[...], scale, shift)
    h_up, h_dn = _shift_rows(h, h_prev, h_next)
    hd = 0.5 * (h_up + h_dn) - h

    w_rkv = w_rkv_ref[...]
    z = _dot(h.astype(BF16), w_rkv)
    halo = jnp.concatenate([h_prev, h_next, jnp.zeros((SUBLANES - 2, d), F32)], axis=0)
    zh = _dot(halo.astype(BF16), w_rkv)
    z_up, z_dn = _shift_rows(z, zh[0:1, :], zh[1:2, :])
    z = z + (0.5 * (z_up + z_dn) - z) * mu_rkv_ref[...]
    r, k, v = z[:, 0:d], z[:, d:2 * d], z[:, 2 * d:3 * d]

    mu = mu_wag_ref[...]
    xw = h + hd * mu[0:1, :]
    xa = h + hd * mu[1:2, :]

    tw = jnp.tanh(_dot(xw.astype(BF16), wd1_ref[...]))
    pre_w = wd0_ref[...] + _dot(tw.astype(BF16), wd2_ref[...])
    lw = -_sigmoid(pre_w) * float(np.exp(-0.5))
    ta = _dot(xa.astype(BF16), a1_ref[...])
    a = _sigmoid(a0_ref[...] + _dot(ta.astype(BF16), a2_ref[...]))

    seg = seg_ref[...]
    kkr = k * kk_w_ref[...]
    kk = kkr * lax.rsqrt(_seg_sum(kkr * kkr, seg) + NORM_EPS)
    ka = ka_ref[...]

    lw0_ref[...] = lw[:, 0:d]
    lw1_ref[...] = lw[:, d:2 * d]
    r_ref[...] = r
    v_ref[...] = v
    kk_ref[...] = kk
    a_f, a_b = a[:, 0:d], a[:, d:2 * d]
    kd0_ref[...] = k * (1.0 + (a_f - 1.0) * ka)
    kd1_ref[...] = k * (1.0 + (a_b - 1.0) * ka)
    b0_ref[...] = kk * a_f
    b1_ref[...] = kk * a_b
    bonus_ref[...] = _seg_sum(r * k * rk_ref[...], seg) * v


def _const_spec(shape):
    nd = len(shape)
    return pl.BlockSpec(shape, lambda b, j: (0,) * nd, pipeline_mode=pl.Buffered(1))


def _token_specs(tm, seq, d):
    tiles8 = tm // SUBLANES
    last8 = seq // SUBLANES - 1
    x_spec = pl.BlockSpec((None, tm, d), lambda b, j: (b, j, 0))
    xp_spec = pl.BlockSpec((None, SUBLANES, d), lambda b, j: (b, jnp.maximum(j * tiles8 - 1, 0), 0))
    xn_spec = pl.BlockSpec((None, SUBLANES, d), lambda b, j: (b, jnp.minimum((j + 1) * tiles8, last8), 0))
    return x_spec, xp_spec, xn_spec


def _pre_call(x, mod, p, consts):
    bsz, seq, d = x.shape
    tm = TOKEN_TILE
    x_spec, xp_spec, xn_spec = _token_specs(tm, seq, d)
    mod_spec = pl.BlockSpec((None, 1, mod.shape[-1]), lambda b, j: (b % mod.shape[0], 0, 0))
    weights = (p["g1"], p["w_rkv"], p["mu_rkv"], p["mu_wag"], p["wd1"], p["wd2"], p["wd0"],
               p["a1"], p["a2"], p["a0"], p["k_k"], p["k_a"], p["r_k"], consts["seg"])
    out = jax.ShapeDtypeStruct((bsz, seq, d), F32)
    return pl.pallas_call(
        _pre_kernel,
        out_shape=(out,) * 10,
        grid=(bsz, seq // tm),
        in_specs=[x_spec, xp_spec, xn_spec, mod_spec] + [_const_spec(w.shape) for w in weights],
        out_specs=(x_spec,) * 10,
        compiler_params=pltpu.CompilerParams(dimension_semantics=("parallel", "parallel"),
                                             vmem_limit_bytes=VMEM_LIMIT),
        name="pre_scan",
    )(x, x, x, mod, *weights)


def _stack(x, lane_head):
    return jnp.concatenate([jnp.where(lane_head == 0, x, 0.0), jnp.where(lane_head == 1, x, 0.0)], axis=0)


def _split2(x):
    hi = x.astype(BF16)
    return hi, (x - hi.astype(F32)).astype(BF16)


def _dot_split(a_hi, a_lo, b_hi, b_lo):
    return _dot(a_hi, b_hi) + _dot(a_lo, b_hi) + _dot(a_hi, b_lo)


def _chunk_prep(insts, cst):
    eye, lane_head = cst["eye"], cst["lane_head"]
    c = insts[0][0].shape[0]
    n = 2 * c
    ids = range(len(insts))
    rev = [t[6] for t in insts]
    cum = [_dot_exact_lhs(cst["tri"][t[6]], t[0]) for t in insts]
    tot = [cum[i][(0 if rev[i] else c - 1):(1 if rev[i] else c), :] for i in ids]
    st = []
    for i in ids:
        lw, r, kd, v, kk, b, _ = insts[i]
        e_in = jnp.exp(cum[i])
        e_ex = jnp.exp(cum[i] - lw)
        e_ng = jnp.exp(-cum[i])
        e_end = jnp.exp(tot[i] - cum[i])
        st.append(dict(
            a=_stack(kk * e_ex, lane_head), r=_stack(r * e_in, lane_head),
            b=_stack(b * e_ng, lane_head).astype(BF16), k=_stack(kd * e_ng, lane_head).astype(BF16),
            bh=_stack(-(b * e_end), lane_head).astype(BF16), kh=_stack(kd * e_end, lane_head).astype(BF16),
            v=_stack(v, lane_head).astype(BF16)))
    ar = [jnp.concatenate([st[i]["a"], st[i]["r"]], axis=0).astype(BF16) for i in ids]
    sb = [_dot_nt(ar[i], st[i]["b"]) for i in ids]
    sk = [_dot_nt(ar[i], st[i]["k"]) for i in ids]
    strict = [cst["strict"][rev[i]] for i in ids]
    incl = [cst["incl"][rev[i]] for i in ids]
    l_m = [jnp.where(strict[i], -sb[i][:n], 0.0) for i in ids]
    m_ak = [jnp.where(strict[i], sk[i][:n], 0.0).astype(BF16) for i in ids]
    m_rb = [jnp.where(incl[i], -sb[i][n:], 0.0).astype(BF16) for i in ids]
    m_rk = [jnp.where(incl[i], sk[i][n:], 0.0).astype(BF16) for i in ids]

    x_m = [jnp.where(eye, 1.0, 0.0) + l_m[i] for i in ids]
    l_sp = [_split2(l_m[i]) for i in ids]
    p_m = [_dot_split(l_sp[i][0], l_sp[i][1], l_sp[i][0], l_sp[i][1]) for i in ids]
    n_sq = int(np.log2(c)) - 1
    for q in range(n_sq):
        if q < n_sq - 1:
            z_sp = [_split2(jnp.concatenate([x_m[i], p_m[i]], axis=0)) for i in ids]
            zz = [_dot_split(z_sp[i][0], z_sp[i][1], z_sp[i][0][n:], z_sp[i][1][n:]) for i in ids]
            x_m = [x_m[i] + zz[i][:n] for i in ids]
            p_m = [zz[i][n:] for i in ids]
        else:
            x_sp = [_split2(x_m[i]) for i in ids]
            p_sp = [_split2(p_m[i]) for i in ids]
            x_m = [x_m[i] + _dot_split(x_sp[i][0], x_sp[i][1], p_sp[i][0], p_sp[i][1]) for i in ids]

    mv = [_dot(m_ak[i], st[i]["v"]) for i in ids]
    w_b = [_dot(x_m[i].astype(BF16), jnp.concatenate([st[i]["a"], mv[i]], axis=1).astype(BF16)).astype(BF16)
           for i in ids]
    ry = [_dot(m_rb[i], w_b[i]) for i in ids]
    yk = [_dot(m_rk[i], st[i]["v"]) for i in ids]
    gh = [_dot_tn(st[i]["bh"], w_b[i]) for i in ids]
    hk = [_dot_tn(st[i]["kh"], st[i]["v"]) for i in ids]
    out = []
    for i in ids:
        r_p = st[i]["r"] + ry[i][:, :n]
        g_m = jnp.where(eye, jnp.exp(tot[i]), 0.0) + gh[i][:, :n]
        y_vs = ry[i][:, n:] + yk[i]
        out.append((jnp.concatenate([r_p, g_m], axis=0).astype(BF16), y_vs[:c] + y_vs[c:], gh[i][:, n:] + hk[i]))
    return out


def _scan_consts(c):
    n = 2 * c
    row = lax.broadcasted_iota(jnp.int32, (n, n), 0)
    col = lax.broadcasted_iota(jnp.int32, (n, n), 1)
    same = (row // c) == (col // c)
    t, s = row % c, col % c
    tr = lax.broadcasted_iota(jnp.int32, (c, c), 0)
    tc = lax.broadcasted_iota(jnp.int32, (c, c), 1)
    lane = lax.broadcasted_iota(jnp.int32, (c, PAIR), 1)
    return {
        "tri": (jnp.where(tc <= tr, 1.0, 0.0).astype(BF16), jnp.where(tc >= tr, 1.0, 0.0).astype(BF16)),
        "strict": (same & (s < t), same & (s > t)),
        "incl": (same & (s <= t), same & (s >= t)),
        "eye": row == col,
        "lane_head": lane // HEAD_DIM,
    }


def _scan_kernel(*refs, has_s0, want_final):
    lw0_ref, lw1_ref, r_ref, kd0_ref, kd1_ref, v_ref, kk_ref, b0_ref, b1_ref = refs[:9]
    pos = 9
    s0_ref = None
    if has_s0:
        s0_ref = refs[pos]
        pos += 1
    y_ref = refs[pos]
    pos += 1
    sf_ref = None
    if want_final:
        sf_ref = refs[pos]
        pos += 1
    yb_ref, st_ref, rg_ref, h_ref = refs[pos:pos + 4]

    c = CHUNK
    seq = r_ref.shape[0]
    nc = seq // c
    cpi = CHUNKS_PER_STEP
    cst = _scan_consts(c)
    zero = jnp.zeros((HEAD_DIM, HEAD_DIM), F32)

    def init_state(dr):
        if not has_s0:
            return jnp.zeros((PAIR, PAIR), F32)
        s_a = s0_ref[dr, 0].T
        s_b = s0_ref[dr, 1].T
        return jnp.concatenate([jnp.concatenate([s_a, zero], axis=1),
                                jnp.concatenate([zero, s_b], axis=1)], axis=0)

    st_ref[0] = init_state(0)
    st_ref[1] = init_state(1)
    lw_refs, kd_refs, b_refs, yv_refs = (lw0_ref, lw1_ref), (kd0_ref, kd1_ref), (b0_ref, b1_ref), (y_ref, yb_ref)

    def prep_body(i, carry):
        insts, dest = [], []
        for j in range(cpi):
            ck = i * cpi + j
            rows = pl.ds(pl.multiple_of(ck * c, c), c)
            r, v, kk = r_ref[rows, :], v_ref[rows, :], kk_ref[rows, :]
            for dr in range(2):
                insts.append((lw_refs[dr][rows, :], r, kd_refs[dr][rows, :], v, kk, b_refs[dr][rows, :], dr))
                dest.append((dr, ck, rows))
        for (dr, ck, rows), (rg, y_v, h_m) in zip(dest, _chunk_prep(insts, cst)):
            rg_ref[dr, ck] = rg
            h_ref[dr, ck] = h_m
            yv_refs[dr][rows, :] = y_v
        return carry

    lax.fori_loop(0, nc // cpi, prep_body, 0)

    def chain_body(i, carry):
        for dr, ck in ((0, i), (1, nc - 1 - i)):
            rows = pl.ds(pl.multiple_of(ck * c, c), c)
            z = _dot(rg_ref[dr, ck], st_ref[dr].astype(BF16))
            yv_refs[dr][rows, :] = yv_refs[dr][rows, :] + (z[:c] + z[c:2 * c])
            st_ref[dr] = z[2 * c:] + h_ref[dr, ck]
        return carry

    lax.fori_loop(0, nc, chain_body, 0)
    s_f, s_b = st_ref[0], st_ref[1]
    y_ref[...] = y_ref[...] + yb_ref[...]
    if want_final:
        for dr, s_bd in ((0, s_f), (1, s_b)):
            sf_ref[dr, 0] = s_bd[:HEAD_DIM, :HEAD_DIM].T
            sf_ref[dr, 1] = s_bd[HEAD_DIM:, HEAD_DIM:].T


def _scan_call(ops, s0, want_final):
    bsz, seq, d = ops[0].shape
    n_pairs = d // PAIR
    nc = seq // CHUNK
    assert nc % CHUNKS_PER_STEP == 0
    col_spec = pl.BlockSpec((None, seq, PAIR), lambda b, q: (b, 0, q))
    st_spec = pl.BlockSpec((None, 2, 2, HEAD_DIM, HEAD_DIM), lambda b, q: (b, 0, q, 0, 0))
    in_specs = [col_spec] * 9
    args = list(ops)
    has_s0 = s0 is not None
    if has_s0:
        in_specs.append(st_spec)
        args.append(s0)
    out_shape = [jax.ShapeDtypeStruct((bsz, seq, d), F32)]
    out_specs = [col_spec]
    if want_final:
        out_shape.append(jax.ShapeDtypeStruct((bsz, 2, N_HEADS, HEAD_DIM, HEAD_DIM), F32))
        out_specs.append(st_spec)
    res = pl.pallas_call(
        functools.partial(_scan_kernel, has_s0=has_s0, want_final=want_final),
        out_shape=tuple(out_shape),
        grid=(bsz, n_pairs),
        in_specs=in_specs,
        out_specs=tuple(out_specs),
        scratch_shapes=[pltpu.VMEM((seq, PAIR), F32), pltpu.VMEM((2, PAIR, PAIR), F32),
                        pltpu.VMEM((2, nc, 2 * PAIR, PAIR), BF16), pltpu.VMEM((2, nc, PAIR, PAIR), F32)],
        compiler_params=pltpu.CompilerParams(dimension_semantics=("parallel", "parallel"),
                                             vmem_limit_bytes=VMEM_LIMIT),
        name="wkv_scan",
    )(*args)
    return res if want_final else (res[0], None)


def _post_kernel(x_ref, xp_ref, xn_ref, mod_ref, y_ref, bonus_ref, g1_ref, g2_ref, gf_ref,
                 w_pg_ref, mu_wag_ref, gw1_ref, gw2_ref, lnw_ref, lnb_ref, pool_w_ref, pool_scale_ref,
                 band_ref, icnt_ref, seg_ref, w_out_ref, w_ff1_ref, w_ff2_ref, o_ref, *, final_norm):
    d = x_ref.shape[-1]
    d_pool = len(POOL_WINDOWS) * POOL_GROUP
    mod = mod_ref[...]
    sh1, sc1, ga1, sh2, sc2, ga2 = (mod[:, i * d:(i + 1) * d] for i in range(N_MOD))
    x = x_ref[...]
    h, h_prev, h_next = _tile_h(x_ref, xp_ref, xn_ref, g1_ref[...], sc1, sh1)
    h_up, h_dn = _shift_rows(h, h_prev, h_next)
    xg = h + (0.5 * (h_up + h_dn) - h) * mu_wag_ref[2:3, :]

    z = _dot(h.astype(BF16), w_pg_ref[...])
    outs = []
    for g in range(len(POOL_WINDOWS)):
        zg = z[:, g * POOL_GROUP:(g + 1) * POOL_GROUP]
        zh = zg.astype(BF16)
        zl = (zg - zh.astype(F32)).astype(BF16)
        band = band_ref[g]
        mixed = (_dot(band, zh) + _dot(band, zl)) * icnt_ref[g] - zg
        outs.append(_dot(mixed.astype(BF16), pool_w_ref[g]))
    a_out = jnp.concatenate(outs, axis=-1) * pool_scale_ref[...]
    gate_a = _sigmoid(z[:, d_pool:d_pool + d])
    gate_b = _sigmoid(z[:, d_pool + d:d_pool + 2 * d])
    gl = _sigmoid(_dot(xg.astype(BF16), gw1_ref[...]))
    g_out = _dot(gl.astype(BF16), gw2_ref[...])

    seg = seg_ref[...]
    y = y_ref[...]
    yc = y - _seg_sum(y, seg) * (1.0 / HEAD_DIM)
    var = _seg_sum(yc * yc, seg) * (1.0 / HEAD_DIM)
    yn = yc * lax.rsqrt(var + LNX_EPS) * lnw_ref[...] + lnb_ref[...]
    b_out = (yn + bonus_ref[...]) * g_out
    mix = _dot((gate_a * a_out + gate_b * b_out).astype(BF16), w_out_ref[...])
    x = x + ga1 * mix

    h2 = _modulated_norm(x, g2_ref[...], sc2, sh2)
    u = jnp.maximum(_dot(h2.astype(BF16), w_ff1_ref[...]), 0.0)
    ff = _dot((u * u).astype(BF16), w_ff2_ref[...])
    x = x + ga2 * ff
    if final_norm:
        x = x * lax.rsqrt(jnp.mean(x * x, axis=-1, keepdims=True) + RMS_EPS) * gf_ref[...]
    o_ref[...] = x


def _post_call(x, mod, y, bonus, p, consts, g_final, final_norm):
    bsz, seq, d = x.shape
    tm = TOKEN_TILE
    x_spec, xp_spec, xn_spec = _token_specs(tm, seq, d)
    mod_spec = pl.BlockSpec((None, 1, mod.shape[-1]), lambda b, j: (b % mod.shape[0], 0, 0))
    weights = (p["g1"], p["g2"], g_final, p["w_pg"], p["mu_wag"], p["gw1"], p["gw2"], p["ln_x_w"],
               p["ln_x_b"], p["pool_w"], p["pool_scale"], consts["band"], consts["icnt"], consts["seg"],
               p["w_out"], p["w_ff1"], p["w_ff2"])
    return pl.pallas_call(
        functools.partial(_post_kernel, final_norm=final_norm),
        out_shape=jax.ShapeDtypeStruct((bsz, seq, d), F32),
        grid=(bsz, seq // tm),
        in_specs=[x_spec, xp_spec, xn_spec, mod_spec, x_spec, x_spec] + [_const_spec(w.shape) for w in weights],
        out_specs=x_spec,
        compiler_params=pltpu.CompilerParams(dimension_semantics=("parallel", "parallel"),
                                             vmem_limit_bytes=VMEM_LIMIT),
        name="post_scan",
    )(x, x, x, mod, y, bonus, *weights)


def _pool_consts(tm, row_len):
    t = np.arange(tm)
    pos, row = t % row_len, t // row_len
    bands, icnts = [], []
    for win in POOL_WINDOWS:
        lo = np.clip(pos - win // 2, 0, row_len)
        hi = np.clip(pos + win - win // 2, 0, row_len)
        inside = (pos[None, :] >= lo[:, None]) & (pos[None, :] < hi[:, None]) & (row[None, :] == row[:, None])
        bands.append(inside.astype(np.float32))
        icnts.append(np.broadcast_to((1.0 / (hi - lo))[:, None], (tm, POOL_GROUP)).astype(np.float32))
    return jnp.asarray(np.stack(bands), BF16), jnp.asarray(np.stack(icnts), F32)


def _block_diag2(w):
    z = jnp.zeros_like(w[0])
    return jnp.concatenate([jnp.concatenate([w[0], z], axis=1), jnp.concatenate([z, w[1]], axis=1)], axis=0)


def _layer_params(li, w):
    d = w["w_in"].shape[1]
    d_pool = len(POOL_WINDOWS) * POOL_GROUP
    w_in = w["w_in"][li].astype(BF16)
    row = lambda a: a.reshape(1, -1)
    return {
        "g1": row(w["g_norm1"][li]), "g2": row(w["g_norm2"][li]),
        "w_rkv": w_in[:, d_pool:d_pool + 3 * d],
        "w_pg": jnp.concatenate([w_in[:, :d_pool], w_in[:, d_pool + 3 * d:]], axis=1),
        "mu_rkv": row(w["mu_rkv"][li]), "mu_wag": w["mu_wag"][li],
        "wd1": jnp.concatenate([w["w_dec1"][li, 0], w["w_dec1"][li, 1]], axis=1).astype(BF16),
        "wd2": _block_diag2(w["w_dec2"][li]).astype(BF16), "wd0": row(w["w_dec0"][li]),
        "a1": jnp.concatenate([w["a1"][li, 0], w["a1"][li, 1]], axis=1).astype(BF16),
        "a2": _block_diag2(w["a2"][li]).astype(BF16), "a0": row(w["a0"][li]),
        "gw1": w["gate_w1"][li].astype(BF16), "gw2": w["gate_w2"][li].astype(BF16),
        "k_k": row(w["k_k"][li]), "k_a": row(w["k_a"][li]), "r_k": row(w["r_k"][li]),
        "ln_x_w": row(w["ln_x_w"][li]), "ln_x_b": row(w["ln_x_b"][li]),
        "pool_w": w["pool_w"][li].astype(BF16), "pool_scale": row(w["pool_scale"][li]),
        "w_out": w["w_out"][li].astype(BF16), "w_ff1": w["w_ff1"][li].astype(BF16),
        "w_ff2": w["w_ff2"][li].astype(BF16),
    }


def _layer(x, mod, s0, want_final, p, consts, g_final, final_norm):
    ops = _pre_call(x, mod, p, consts)
    y, s_fin = _scan_call(ops[:9], s0, want_final)
    return _post_call(x, mod, y, ops[9], p, consts, g_final, final_norm), s_fin


def kernel(x_prompt, x_sample, state_rwkv, c, c_ctx, w_ada, b_ada, g_norm1, g_norm2, w_in, mu_rkv, mu_wag, w_dec0, w_dec1, w_dec2, a0, a1, a2, gate_w1, gate_w2, k_k, k_a, r_k, ln_x_w, ln_x_b, pool_w, pool_scale, w_out, w_ff1, w_ff2, g_final):
    w = dict(w_ada=w_ada, b_ada=b_ada, g_norm1=g_norm1, g_norm2=g_norm2, w_in=w_in, mu_rkv=mu_rkv,
             mu_wag=mu_wag, w_dec0=w_dec0, w_dec1=w_dec1, w_dec2=w_dec2, a0=a0, a1=a1, a2=a2,
             gate_w1=gate_w1, gate_w2=gate_w2, k_k=k_k, k_a=k_a, r_k=r_k, ln_x_w=ln_x_w, ln_x_b=ln_x_b,
             pool_w=pool_w, pool_scale=pool_scale, w_out=w_out, w_ff1=w_ff1, w_ff2=w_ff2)
    depth = w_in.shape[0]
    n_lat, lat_len, d = x_sample.shape
    ctx_len = x_prompt.shape[1]
    assert ctx_len % TOKEN_TILE == 0 and lat_len % TOKEN_TILE == 0 and TOKEN_TILE % GRID_W == 0
    assert ctx_len == TOKEN_TILE, "context pooling runs over the whole sequence inside one token tile"

    seg = jnp.asarray(np.kron(np.eye(SEG_TILE // HEAD_DIM), np.ones((HEAD_DIM, HEAD_DIM))), BF16)
    band_c, icnt_c = _pool_consts(TOKEN_TILE, ctx_len)
    band_l, icnt_l = _pool_consts(TOKEN_TILE, GRID_W)
    consts_ctx = {"seg": seg, "band": band_c, "icnt": icnt_c}
    consts_lat = {"seg": seg, "band": band_l, "icnt": icnt_l}

    n_cv = 1 + n_lat
    cv = jnp.concatenate([c_ctx[None, :], c, jnp.zeros((-n_cv % SUBLANES, d), F32)], axis=0)
    g_fin = g_final.reshape(1, d)
    ctx, lat = x_prompt, x_sample
    new_states = []
    for li in range(depth):
        p = _layer_params(li, w)
        mod = _ada_call(cv, w_ada[li], b_ada[li].reshape(1, -1))
        mod_ctx = mod[0:1].reshape(1, 1, -1)
        mod_lat = mod[1:n_cv].reshape(n_lat, 1, -1)
        last = li == depth - 1
        ctx, s_fin = _layer(ctx, mod_ctx, None, True, p, consts_ctx, g_fin, last)
        new_states.append(s_fin)
        lat, _ = _layer(lat, mod_lat, state_rwkv[:, li], False, p, consts_lat, g_fin, last)
    return ctx, lat, jnp.stack(new_states, axis=1)
```

```python
import functools

import numpy as np
import jax
import jax.numpy as jnp
from jax import lax
from jax.experimental import pallas as pl
from jax.experimental.pallas import tpu as pltpu

F32 = jnp.float32
BF16 = jnp.bfloat16

N_HEADS = 16
HEAD_DIM = 64
PAIR = 2 * HEAD_DIM
POOL_WINDOWS = (2, 4, 8, 16)
POOL_GROUP = 128
GRID_W = 64
N_MOD = 6
RMS_EPS = 1e-6
LNX_EPS = 64e-5
NORM_EPS = 1e-12

CHUNK = 64
CHUNKS_PER_STEP = 4
TOKEN_TILE = 256
SEG_TILE = 256
SUBLANES = 8
VMEM_LIMIT = 56 * 1024 * 1024


def _dot(a, b):
    return jnp.dot(a, b, preferred_element_type=F32)


def _dot_nt(a, b):
    return lax.dot_general(a, b, (((1,), (1,)), ((), ())), preferred_element_type=F32)


def _dot_tn(a, b):
    return lax.dot_general(a, b, (((0,), (0,)), ((), ())), preferred_element_type=F32)


def _dot_f32(a, b):
    return jnp.dot(a, b, precision=lax.Precision.HIGHEST, preferred_element_type=F32)


def _split3(x):
    h1 = x.astype(BF16)
    r1 = x - h1.astype(F32)
    h2 = r1.astype(BF16)
    h3 = (r1 - h2.astype(F32)).astype(BF16)
    return h1, h2, h3


def _dot_exact_rhs(x, m):
    h1, h2, h3 = _split3(x)
    return _dot(h1, m) + _dot(h2, m) + _dot(h3, m)


def _dot_exact_lhs(m, x):
    h1, h2, h3 = _split3(x)
    return _dot(m, h1) + _dot(m, h2) + _dot(m, h3)


def _seg_sum(x, seg_ones):
    d = x.shape[-1]
    parts = [_dot_exact_rhs(x[:, q:q + SEG_TILE], seg_ones) for q in range(0, d, SEG_TILE)]
    return jnp.concatenate(parts, axis=-1)


def _sigmoid(x):
    return 1.0 / (1.0 + jnp.exp(-x))


def _modulated_norm(x, g, scale, shift):
    y = x * lax.rsqrt(jnp.mean(x * x, axis=-1, keepdims=True) + RMS_EPS)
    return (y * g) * (1.0 + scale) + shift


def _shift_rows(v, first_row, last_row):
    n = v.shape[0]
    row = lax.broadcasted_iota(jnp.int32, v.shape, 0)
    up = jnp.where(row == 0, first_row, pltpu.roll(v, 1, 0))
    dn = jnp.where(row == n - 1, last_row, pltpu.roll(v, n - 1, 0))
    return up, dn


def _tile_h(x_ref, xp_ref, xn_ref, g, scale, shift):
    j = pl.program_id(1)
    nj = pl.num_programs(1)
    h = _modulated_norm(x_ref[...], g, scale, shift)
    halo = jnp.concatenate([xp_ref[SUBLANES - 1:SUBLANES, :], xn_ref[0:1, :]], axis=0)
    hh = _modulated_norm(halo, g, scale, shift)
    h_prev = jnp.where(j > 0, hh[0:1, :], 0.0)
    h_next = jnp.where(j < nj - 1, hh[1:2, :], 0.0)
    return h, h_prev, h_next


def _ada_kernel(cv_ref, w_ref, b_ref, o_ref):
    cv = cv_ref[...]
    o_ref[...] = _dot_f32(cv * _sigmoid(cv), w_ref[...]) + b_ref[...]


def _ada_call(cv, w_ada, b_ada):
    rows, d = cv.shape
    n = w_ada.shape[1]
    bn = n // 4
    return pl.pallas_call(
        _ada_kernel,
        out_shape=jax.ShapeDtypeStruct((rows, n), F32),
        grid=(n // bn,),
        in_specs=[pl.BlockSpec((rows, d), lambda i: (0, 0)),
                  pl.BlockSpec((d, bn), lambda i: (0, i)),
                  pl.BlockSpec((1, bn), lambda i: (0, i))],
        out_specs=pl.BlockSpec((rows, bn), lambda i: (0, i)),
        compiler_params=pltpu.CompilerParams(dimension_semantics=("arbitrary",),
                                             vmem_limit_bytes=VMEM_LIMIT),
        name="ada_mod",
    )(cv, w_ada, b_ada)


def _pre_kernel(x_ref, xp_ref, xn_ref, mod_ref, g1_ref, w_rkv_ref, mu_rkv_ref, mu_wag_ref,
                wd1_ref, wd2_ref, wd0_ref, a1_ref, a2_ref, a0_ref, kk_w_ref, ka_ref, rk_ref,
                seg_ref,
                lw0_ref, lw1_ref, r_ref, kd0_ref, kd1_ref, v_ref, kk_ref, b0_ref, b1_ref,
                bonus_ref):
    d = x_ref.shape[-1]
    mod = mod_ref[...]
    shift, scale = mod[:, 0:d], mod[:, d:2 * d]
    h, h_prev, h_next = _tile_h(x_ref, xp_ref, xn_ref, g1_ref[...], scale, shift)
    h_up, h_dn = _shift_rows(h, h_prev, h_next)
    hd = 0.5 * (h_up + h_dn) - h

    w_rkv = w_rkv_ref[...]
    z = _dot(h.astype(BF16), w_rkv)
    halo = jnp.concatenate([h_prev, h_next, jnp.zeros((SUBLANES - 2, d), F32)], axis=0)
    zh = _dot(halo.astype(BF16), w_rkv)
    z_up, z_dn = _shift_rows(z, zh[0:1, :], zh[1:2, :])
    z = z + (0.5 * (z_up + z_dn) - z) * mu_rkv_ref[...]
    r, k, v = z[:, 0:d], z[:, d:2 * d], z[:, 2 * d:3 * d]

    mu = mu_wag_ref[...]
    xw = h + hd * mu[0:1, :]
    xa = h + hd * mu[1:2, :]

    tw = jnp.tanh(_dot(xw.astype(BF16), wd1_ref[...]))
    pre_w = wd0_ref[...] + _dot(tw.astype(BF16), wd2_ref[...])
    lw = -_sigmoid(pre_w) * float(np.exp(-0.5))
    ta = _dot(xa.astype(BF16), a1_ref[...])
    a = _sigmoid(a0_ref[...] + _dot(ta.astype(BF16), a2_ref[...]))

    seg = seg_ref[...]
    kkr = k * kk_w_ref[...]
    kk = kkr * lax.rsqrt(_seg_sum(kkr * kkr, seg) + NORM_EPS)
    ka = ka_ref[...]

    lw0_ref[...] = lw[:, 0:d]
    lw1_ref[...] = lw[:, d:2 * d]
    r_ref[...] = r
    v_ref[...] = v
    kk_ref[...] = kk
    a_f, a_b = a[:, 0:d], a[:, d:2 * d]
    kd0_ref[...] = k * (1.0 + (a_f - 1.0) * ka)
    kd1_ref[...] = k * (1.0 + (a_b - 1.0) * ka)
    b0_ref[...] = kk * a_f
    b1_ref[...] = kk * a_b
    bonus_ref[...] = _seg_sum(r * k * rk_ref[...], seg) * v


def _const_spec(shape):
    nd = len(shape)
    return pl.BlockSpec(shape, lambda b, j: (0,) * nd, pipeline_mode=pl.Buffered(1))


def _token_specs(tm, seq, d):
    tiles8 = tm // SUBLANES
    last8 = seq // SUBLANES - 1
    x_spec = pl.BlockSpec((None, tm, d), lambda b, j: (b, j, 0))
    xp_spec = pl.BlockSpec((None, SUBLANES, d), lambda b, j: (b, jnp.maximum(j * tiles8 - 1, 0), 0))
    xn_spec = pl.BlockSpec((None, SUBLANES, d), lambda b, j: (b, jnp.minimum((j + 1) * tiles8, last8), 0))
    return x_spec, xp_spec, xn_spec


def _pre_call(x, mod, p, consts):
    bsz, seq, d = x.shape
    tm = TOKEN_TILE
    x_spec, xp_spec, xn_spec = _token_specs(tm, seq, d)
    mod_spec = pl.BlockSpec((None, 1, mod.shape[-1]), lambda b, j: (b % mod.shape[0], 0, 0))
    weights = (p["g1"], p["w_rkv"], p["mu_rkv"], p["mu_wag"], p["wd1"], p["wd2"], p["wd0"],
               p["a1"], p["a2"], p["a0"], p["k_k"], p["k_a"], p["r_k"], consts["seg"])
    out = jax.ShapeDtypeStruct((bsz, seq, d), F32)
    return pl.pallas_call(
        _pre_kernel,
        out_shape=(out,) * 10,
        grid=(bsz, seq // tm),
        in_specs=[x_spec, xp_spec, xn_spec, mod_spec] + [_const_spec(w.shape) for w in weights],
        out_specs=(x_spec,) * 10,
        compiler_params=pltpu.CompilerParams(dimension_semantics=("parallel", "parallel"),
                                             vmem_limit_bytes=VMEM_LIMIT),
        name="pre_scan",
    )(x, x, x, mod, *weights)


def _stack(x, lane_head):
    return jnp.concatenate([jnp.where(lane_head == 0, x, 0.0), jnp.where(lane_head == 1, x, 0.0)], axis=0)


def _split2(x):
    hi = x.astype(BF16)
    return hi, (x - hi.astype(F32)).astype(BF16)


def _dot_split(a_hi, a_lo, b_hi, b_lo):
    return _dot(a_hi, b_hi) + _dot(a_lo, b_hi) + _dot(a_hi, b_lo)


def _chunk_prep(insts, cst):
    eye, lane_head = cst["eye"], cst["lane_head"]
    c = insts[0][0].shape[0]
    n = 2 * c
    ids = range(len(insts))
    rev = [t[6] for t in insts]
    cum = [_dot_exact_lhs(cst["tri"][t[6]], t[0]) for t in insts]
    tot = [cum[i][(0 if rev[i] else c - 1):(1 if rev[i] else c), :] for i in ids]
    st = []
    for i in ids:
        lw, r, kd, v, kk, b, _ = insts[i]
        e_in = jnp.exp(cum[i])
        e_ex = jnp.exp(cum[i] - lw)
        e_ng = jnp.exp(-cum[i])
        e_end = jnp.exp(tot[i] - cum[i])
        st.append(dict(
            a=_stack(kk * e_ex, lane_head), r=_stack(r * e_in, lane_head),
            b=_stack(b * e_ng, lane_head).astype(BF16), k=_stack(kd * e_ng, lane_head).astype(BF16),
            bh=_stack(-(b * e_end), lane_head).astype(BF16), kh=_stack(kd * e_end, lane_head).astype(BF16),
            v=_stack(v, lane_head).astype(BF16)))
    ar = [jnp.concatenate([st[i]["a"], st[i]["r"]], axis=0).astype(BF16) for i in ids]
    sb = [_dot_nt(ar[i], st[i]["b"]) for i in ids]
    sk = [_dot_nt(ar[i], st[i]["k"]) for i in ids]
    strict = [cst["strict"][rev[i]] for i in ids]
    incl = [cst["incl"][rev[i]] for i in ids]
    l_m = [jnp.where(strict[i], -sb[i][:n], 0.0) for i in ids]
    m_ak = [jnp.where(strict[i], sk[i][:n], 0.0).astype(BF16) for i in ids]
    m_rb = [jnp.where(incl[i], -sb[i][n:], 0.0).astype(BF16) for i in ids]
    m_rk = [jnp.where(incl[i], sk[i][n:], 0.0).astype(BF16) for i in ids]

    eye_f = jnp.where(eye, 1.0, 0.0)
    x_m = [eye_f + l_m[i] for i in ids]
    l_sp = [_split2(l_m[i]) for i in ids]
    p_b = [_dot(l_sp[i][0], l_sp[i][0]).astype(BF16) for i in ids]
    n_sq = int(np.log2(c)) - 1
    for q in range(n_sq):
        if q < n_sq - 1:
            zz = [_dot(jnp.concatenate([x_m[i].astype(BF16), p_b[i]], axis=0), p_b[i]) for i in ids]
            x_m = [x_m[i] + zz[i][:n] for i in ids]
            p_b = [zz[i][n:].astype(BF16) for i in ids]
        else:
            x_m = [x_m[i] + _dot(x_m[i].astype(BF16), p_b[i]) for i in ids]
    t_sp = [_split2(x_m[i]) for i in ids]
    e_m = [(eye_f - x_m[i]) + _dot_split(l_sp[i][0], l_sp[i][1], t_sp[i][0], t_sp[i][1]) for i in ids]
    x_m = [x_m[i] + _dot(t_sp[i][0], e_m[i].astype(BF16)) for i in ids]

    mv = [_dot(m_ak[i], st[i]["v"]) for i in ids]
    w_b = [_dot(x_m[i].astype(BF16), jnp.concatenate([st[i]["a"], mv[i]], axis=1).astype(BF16)).astype(BF16)
           for i in ids]
    ry = [_dot(m_rb[i], w_b[i]) for i in ids]
    yk = [_dot(m_rk[i], st[i]["v"]) for i in ids]
    gh = [_dot_tn(st[i]["bh"], w_b[i]) for i in ids]
    hk = [_dot_tn(st[i]["kh"], st[i]["v"]) for i in ids]
    out = []
    for i in ids:
        r_p = st[i]["r"] + ry[i][:, :n]
        g_m = jnp.where(eye, jnp.exp(tot[i]), 0.0) + gh[i][:, :n]
        y_vs = ry[i][:, n:] + yk[i]
        out.append((jnp.concatenate([r_p, g_m], axis=0).astype(BF16), y_vs[:c] + y_vs[c:], gh[i][:, n:] + hk[i]))
    return out


def _scan_consts(c):
    n = 2 * c
    row = lax.broadcasted_iota(jnp.int32, (n, n), 0)
    col = lax.broadcasted_iota(jnp.int32, (n, n), 1)
    same = (row // c) == (col // c)
    t, s = row % c, col % c
    tr = lax.broadcasted_iota(jnp.int32, (c, c), 0)
    tc = lax.broadcasted_iota(jnp.int32, (c, c), 1)
    lane = lax.broadcasted_iota(jnp.int32, (c, PAIR), 1)
    return {
        "tri": (jnp.where(tc <= tr, 1.0, 0.0).astype(BF16), jnp.where(tc >= tr, 1.0, 0.0).astype(BF16)),
        "strict": (same & (s < t), same & (s > t)),
        "incl": (same & (s <= t), same & (s >= t)),
        "eye": row == col,
        "lane_head": lane // HEAD_DIM,
    }


def _scan_kernel(*refs, has_s0, want_final):
    lw0_ref, lw1_ref, r_ref, kd0_ref, kd1_ref, v_ref, kk_ref, b0_ref, b1_ref = refs[:9]
    pos = 9
    s0_ref = None
    if has_s0:
        s0_ref = refs[pos]
        pos += 1
    y_ref = refs[pos]
    pos += 1
    sf_ref = None
    if want_final:
        sf_ref = refs[pos]
        pos += 1
    yb_ref, st_ref, rg_ref, h_ref = refs[pos:pos + 4]

    c = CHUNK
    seq = r_ref.shape[0]
    nc = seq // c
    cpi = CHUNKS_PER_STEP
    cst = _scan_consts(c)
    zero = jnp.zeros((HEAD_DIM, HEAD_DIM), F32)

    def init_state(dr):
        if not has_s0:
            return jnp.zeros((PAIR, PAIR), F32)
        s_a = s0_ref[dr, 0].T
        s_b = s0_ref[dr, 1].T
        return jnp.concatenate([jnp.concatenate([s_a, zero], axis=1),
                                jnp.concatenate([zero, s_b], axis=1)], axis=0)

    st_ref[0] = init_state(0)
    st_ref[1] = init_state(1)
    lw_refs, kd_refs, b_refs, yv_refs = (lw0_ref, lw1_ref), (kd0_ref, kd1_ref), (b0_ref, b1_ref), (y_ref, yb_ref)

    def prep_body(i, carry):
        insts, dest = [], []
        for j in range(cpi):
            ck = i * cpi + j
            rows = pl.ds(pl.multiple_of(ck * c, c), c)
            r, v, kk = r_ref[rows, :], v_ref[rows, :], kk_ref[rows, :]
            for dr in range(2):
                insts.append((lw_refs[dr][rows, :], r, kd_refs[dr][rows, :], v, kk, b_refs[dr][rows, :], dr))
                dest.append((dr, ck, rows))
        for (dr, ck, rows), (rg, y_v, h_m) in zip(dest, _chunk_prep(insts, cst)):
            rg_ref[dr, ck] = rg
            h_ref[dr, ck] = h_m
            yv_refs[dr][rows, :] = y_v
        return carry

    lax.fori_loop(0, nc // cpi, prep_body, 0)

    def chain_body(i, carry):
        for dr, ck in ((0, i), (1, nc - 1 - i)):
            rows = pl.ds(pl.multiple_of(ck * c, c), c)
            z = _dot(rg_ref[dr, ck], st_ref[dr].astype(BF16))
            yv_refs[dr][rows, :] = yv_refs[dr][rows, :] + (z[:c] + z[c:2 * c])
            st_ref[dr] = z[2 * c:] + h_ref[dr, ck]
        return carry

    lax.fori_loop(0, nc, chain_body, 0)
    s_f, s_b = st_ref[0], st_ref[1]
    y_ref[...] = y_ref[...] + yb_ref[...]
    if want_final:
        for dr, s_bd in ((0, s_f), (1, s_b)):
            sf_ref[dr, 0] = s_bd[:HEAD_DIM, :HEAD_DIM].T
            sf_ref[dr, 1] = s_bd[HEAD_DIM:, HEAD_DIM:].T


def _scan_call(ops, s0, want_final):
    bsz, seq, d = ops[0].shape
    n_pairs = d // PAIR
    nc = seq // CHUNK
    assert nc % CHUNKS_PER_STEP == 0
    col_spec = pl.BlockSpec((None, seq, PAIR), lambda b, q: (b, 0, q))
    st_spec = pl.BlockSpec((None, 2, 2, HEAD_DIM, HEAD_DIM), lambda b, q: (b, 0, q, 0, 0))
    in_specs = [col_spec] * 9
    args = list(ops)
    has_s0 = s0 is not None
    if has_s0:
        in_specs.append(st_spec)
        args.append(s0)
    out_shape = [jax.ShapeDtypeStruct((bsz, seq, d), F32)]
    out_specs = [col_spec]
    if want_final:
        out_shape.append(jax.ShapeDtypeStruct((bsz, 2, N_HEADS, HEAD_DIM, HEAD_DIM), F32))
        out_specs.append(st_spec)
    res = pl.pallas_call(
        functools.partial(_scan_kernel, has_s0=has_s0, want_final=want_final),
        out_shape=tuple(out_shape),
        grid=(bsz, n_pairs),
        in_specs=in_specs,
        out_specs=tuple(out_specs),
        scratch_shapes=[pltpu.VMEM((seq, PAIR), F32), pltpu.VMEM((2, PAIR, PAIR), F32),
                        pltpu.VMEM((2, nc, 2 * PAIR, PAIR), BF16), pltpu.VMEM((2, nc, PAIR, PAIR), F32)],
        compiler_params=pltpu.CompilerParams(dimension_semantics=("parallel", "parallel"),
                                             vmem_limit_bytes=VMEM_LIMIT),
        name="wkv_scan",
    )(*args)
    return res if want_final else (res[0], None)


def _post_kernel(x_ref, xp_ref, xn_ref, mod_ref, y_ref, bonus_ref, g1_ref, g2_ref, gf_ref,
                 w_pg_ref, mu_wag_ref, gw1_ref, gw2_ref, lnw_ref, lnb_ref, pool_w_ref, pool_scale_ref,
                 band_ref, icnt_ref, seg_ref, w_out_ref, w_ff1_ref, w_ff2_ref, o_ref, *, final_norm):
    d = x_ref.shape[-1]
    d_pool = len(POOL_WINDOWS) * POOL_GROUP
    mod = mod_ref[...]
    sh1, sc1, ga1, sh2, sc2, ga2 = (mod[:, i * d:(i + 1) * d] for i in range(N_MOD))
    x = x_ref[...]
    h, h_prev, h_next = _tile_h(x_ref, xp_ref, xn_ref, g1_ref[...], sc1, sh1)
    h_up, h_dn = _shift_rows(h, h_prev, h_next)
    xg = h + (0.5 * (h_up + h_dn) - h) * mu_wag_ref[2:3, :]

    z = _dot(h.astype(BF16), w_pg_ref[...])
    outs = []
    for g in range(len(POOL_WINDOWS)):
        zg = z[:, g * POOL_GROUP:(g + 1) * POOL_GROUP]
        zh = zg.astype(BF16)
        zl = (zg - zh.astype(F32)).astype(BF16)
        band = band_ref[g]
        mixed = (_dot(band, zh) + _dot(band, zl)) * icnt_ref[g] - zg
        outs.append(_dot(mixed.astype(BF16), pool_w_ref[g]))
    a_out = jnp.concatenate(outs, axis=-1) * pool_scale_ref[...]
    gate_a = _sigmoid(z[:, d_pool:d_pool + d])
    gate_b = _sigmoid(z[:, d_pool + d:d_pool + 2 * d])
    gl = _sigmoid(_dot(xg.astype(BF16), gw1_ref[...]))
    g_out = _dot(gl.astype(BF16), gw2_ref[...])

    seg = seg_ref[...]
    y = y_ref[...]
    yc = y - _seg_sum(y, seg) * (1.0 / HEAD_DIM)
    var = _seg_sum(yc * yc, seg) * (1.0 / HEAD_DIM)
    yn = yc * lax.rsqrt(var + LNX_EPS) * lnw_ref[...] + lnb_ref[...]
    b_out = (yn + bonus_ref[...]) * g_out
    mix = _dot((gate_a * a_out + gate_b * b_out).astype(BF16), w_out_ref[...])
    x = x + ga1 * mix

    h2 = _modulated_norm(x, g2_ref[...], sc2, sh2)
    u = jnp.maximum(_dot(h2.astype(BF16), w_ff1_ref[...]), 0.0)
    ff = _dot((u * u).astype(BF16), w_ff2_ref[...])
    x = x + ga2 * ff
    if final_norm:
        x = x * lax.rsqrt(jnp.mean(x * x, axis=-1, keepdims=True) + RMS_EPS) * gf_ref[...]
    o_ref[...] = x


def _post_call(x, mod, y, bonus, p, consts, g_final, final_norm):
    bsz, seq, d = x.shape
    tm = TOKEN_TILE
    x_spec, xp_spec, xn_spec = _token_specs(tm, seq, d)
    mod_spec = pl.BlockSpec((None, 1, mod.shape[-1]), lambda b, j: (b % mod.shape[0], 0, 0))
    weights = (p["g1"], p["g2"], g_final, p["w_pg"], p["mu_wag"], p["gw1"], p["gw2"], p["ln_x_w"],
               p["ln_x_b"], p["pool_w"], p["pool_scale"], consts["band"], consts["icnt"], consts["seg"],
               p["w_out"], p["w_ff1"], p["w_ff2"])
    return pl.pallas_call(
        functools.partial(_post_kernel, final_norm=final_norm),
        out_shape=jax.ShapeDtypeStruct((bsz, seq, d), F32),
        grid=(bsz, seq // tm),
        in_specs=[x_spec, xp_spec, xn_spec, mod_spec, x_spec, x_spec] + [_const_spec(w.shape) for w in weights],
        out_specs=x_spec,
        compiler_params=pltpu.CompilerParams(dimension_semantics=("parallel", "parallel"),
                                             vmem_limit_bytes=VMEM_LIMIT),
        name="post_scan",
    )(x, x, x, mod, y, bonus, *weights)


def _pool_consts(tm, row_len):
    t = np.arange(tm)
    pos, row = t % row_len, t // row_len
    bands, icnts = [], []
    for win in POOL_WINDOWS:
        lo = np.clip(pos - win // 2, 0, row_len)
        hi = np.clip(pos + win - win // 2, 0, row_len)
        inside = (pos[None, :] >= lo[:, None]) & (pos[None, :] < hi[:, None]) & (row[None, :] == row[:, None])
        bands.append(inside.astype(np.float32))
        icnts.append(np.broadcast_to((1.0 / (hi - lo))[:, None], (tm, POOL_GROUP)).astype(np.float32))
    return jnp.asarray(np.stack(bands), BF16), jnp.asarray(np.stack(icnts), F32)


def _block_diag2(w):
    z = jnp.zeros_like(w[0])
    return jnp.concatenate([jnp.concatenate([w[0], z], axis=1), jnp.concatenate([z, w[1]], axis=1)], axis=0)


def _layer_params(li, w):
    d = w["w_in"].shape[1]
    d_pool = len(POOL_WINDOWS) * POOL_GROUP
    w_in = w["w_in"][li].astype(BF16)
    row = lambda a: a.reshape(1, -1)
    return {
        "g1": row(w["g_norm1"][li]), "g2": row(w["g_norm2"][li]),
        "w_rkv": w_in[:, d_pool:d_pool + 3 * d],
        "w_pg": jnp.concatenate([w_in[:, :d_pool], w_in[:, d_pool + 3 * d:]], axis=1),
        "mu_rkv": row(w["mu_rkv"][li]), "mu_wag": w["mu_wag"][li],
        "wd1": jnp.concatenate([w["w_dec1"][li, 0], w["w_dec1"][li, 1]], axis=1).astype(BF16),
        "wd2": _block_diag2(w["w_dec2"][li]).astype(BF16), "wd0": row(w["w_dec0"][li]),
        "a1": jnp.concatenate([w["a1"][li, 0], w["a1"][li, 1]], axis=1).astype(BF16),
        "a2": _block_diag2(w["a2"][li]).astype(BF16), "a0": row(w["a0"][li]),
        "gw1": w["gate_w1"][li].astype(BF16), "gw2": w["gate_w2"][li].astype(BF16),
        "k_k": row(w["k_k"][li]), "k_a": row(w["k_a"][li]), "r_k": row(w["r_k"][li]),
        "ln_x_w": row(w["ln_x_w"][li]), "ln_x_b": row(w["ln_x_b"][li]),
        "pool_w": w["pool_w"][li].astype(BF16), "pool_scale": row(w["pool_scale"][li]),
        "w_out": w["w_out"][li].astype(BF16), "w_ff1": w["w_ff1"][li].astype(BF16),
        "w_ff2": w["w_ff2"][li].astype(BF16),
    }


def _layer(x, mod, s0, want_final, p, consts, g_final, final_norm):
    ops = _pre_call(x, mod, p, consts)
    y, s_fin = _scan_call(ops[:9], s0, want_final)
    return _post_call(x, mod, y, ops[9], p, consts, g_final, final_norm), s_fin


def kernel(x_prompt, x_sample, state_rwkv, c, c_ctx, w_ada, b_ada, g_norm1, g_norm2, w_in, mu_rkv, mu_wag, w_dec0, w_dec1, w_dec2, a0, a1, a2, gate_w1, gate_w2, k_k, k_a, r_k, ln_x_w, ln_x_b, pool_w, pool_scale, w_out, w_ff1, w_ff2, g_final):
    w = dict(w_ada=w_ada, b_ada=b_ada, g_norm1=g_norm1, g_norm2=g_norm2, w_in=w_in, mu_rkv=mu_rkv,
             mu_wag=mu_wag, w_dec0=w_dec0, w_dec1=w_dec1, w_dec2=w_dec2, a0=a0, a1=a1, a2=a2,
             gate_w1=gate_w1, gate_w2=gate_w2, k_k=k_k, k_a=k_a, r_k=r_k, ln_x_w=ln_x_w, ln_x_b=ln_x_b,
             pool_w=pool_w, pool_scale=pool_scale, w_out=w_out, w_ff1=w_ff1, w_ff2=w_ff2)
    depth = w_in.shape[0]
    n_lat, lat_len, d = x_sample.shape
    ctx_len = x_prompt.shape[1]
    assert ctx_len % TOKEN_TILE == 0 and lat_len % TOKEN_TILE == 0 and TOKEN_TILE % GRID_W == 0
    assert ctx_len == TOKEN_TILE, "context pooling runs over the whole sequence inside one token tile"

    seg = jnp.asarray(np.kron(np.eye(SEG_TILE // HEAD_DIM), np.ones((HEAD_DIM, HEAD_DIM))), BF16)
    band_c, icnt_c = _pool_consts(TOKEN_TILE, ctx_len)
    band_l, icnt_l = _pool_consts(TOKEN_TILE, GRID_W)
    consts_ctx = {"seg": seg, "band": band_c, "icnt": icnt_c}
    consts_lat = {"seg": seg, "band": band_l, "icnt": icnt_l}

    n_cv = 1 + n_lat
    cv = jnp.concatenate([c_ctx[None, :], c, jnp.zeros((-n_cv % SUBLANES, d), F32)], axis=0)
    g_fin = g_final.reshape(1, d)
    ctx, lat = x_prompt, x_sample
    new_states = []
    for li in range(depth):
        p = _layer_params(li, w)
        mod = _ada_call(cv, w_ada[li], b_ada[li].reshape(1, -1))
        mod_ctx = mod[0:1].reshape(1, 1, -1)
        mod_lat = mod[1:n_cv].reshape(n_lat, 1, -1)
        last = li == depth - 1
        ctx, s_fin = _layer(ctx, mod_ctx, None, True, p, consts_ctx, g_fin, last)
        new_states.append(s_fin)
        lat, _ = _layer(lat, mod_lat, state_rwkv[:, li], False, p, consts_lat, g_fin, last)
    return ctx, lat, jnp.stack(new_states, axis=1)
```

```python
import functools

import numpy as np
import jax
import jax.numpy as jnp
from jax import lax
from jax.experimental import pallas as pl
from jax.experimental.pallas import tpu as pltpu

F32 = jnp.float32
BF16 = jnp.bfloat16

N_HEADS = 16
HEAD_DIM = 64
PAIR = 2 * HEAD_DIM
POOL_WINDOWS = (2, 4, 8, 16)
POOL_GROUP = 128
GRID_W = 64
N_MOD = 6
RMS_EPS = 1e-6
LNX_EPS = 64e-5
NORM_EPS = 1e-12

CHUNK = 64
ITEMS_PER_TRIP = 8
SCAN_VMEM_BUDGET = 46 * 1024 * 1024
TOKEN_TILE = 256
SEG_TILE = 256
SUBLANES = 8
VMEM_LIMIT = 56 * 1024 * 1024


def _dot(a, b):
    return jnp.dot(a, b, preferred_element_type=F32)


def _dot_nt(a, b):
    return lax.dot_general(a, b, (((1,), (1,)), ((), ())), preferred_element_type=F32)


def _dot_tn(a, b):
    return lax.dot_general(a, b, (((0,), (0,)), ((), ())), preferred_element_type=F32)


def _dot_f32(a, b):
    return jnp.dot(a, b, precision=lax.Precision.HIGHEST, preferred_element_type=F32)


def _split3(x):
    h1 = x.astype(BF16)
    r1 = x - h1.astype(F32)
    h2 = r1.astype(BF16)
    h3 = (r1 - h2.astype(F32)).astype(BF16)
    return h1, h2, h3


def _dot_exact_rhs(x, m):
    h1, h2, h3 = _split3(x)
    return _dot(h1, m) + _dot(h2, m) + _dot(h3, m)


def _dot_exact_lhs(m, x):
    h1, h2, h3 = _split3(x)
    return _dot(m, h1) + _dot(m, h2) + _dot(m, h3)


def _seg_sum(x, seg_ones):
    d = x.shape[-1]
    parts = [_dot_exact_rhs(x[:, q:q + SEG_TILE], seg_ones) for q in range(0, d, SEG_TILE)]
    return jnp.concatenate(parts, axis=-1)


def _sigmoid(x):
    return 1.0 / (1.0 + jnp.exp(-x))


def _modulated_norm(x, g, scale, shift):
    y = x * lax.rsqrt(jnp.mean(x * x, axis=-1, keepdims=True) + RMS_EPS)
    return (y * g) * (1.0 + scale) + shift


def _shift_rows(v, first_row, last_row):
    n = v.shape[0]
    row = lax.broadcasted_iota(jnp.int32, v.shape, 0)
    up = jnp.where(row == 0, first_row, pltpu.roll(v, 1, 0))
    dn = jnp.where(row == n - 1, last_row, pltpu.roll(v, n - 1, 0))
    return up, dn


def _tile_h(x_ref, xp_ref, xn_ref, g, scale, shift):
    j = pl.program_id(1)
    nj = pl.num_programs(1)
    h = _modulated_norm(x_ref[...], g, scale, shift)
    halo = jnp.concatenate([xp_ref[SUBLANES - 1:SUBLANES, :], xn_ref[0:1, :]], axis=0)
    hh = _modulated_norm(halo, g, scale, shift)
    h_prev = jnp.where(j > 0, hh[0:1, :], 0.0)
    h_next = jnp.where(j < nj - 1, hh[1:2, :], 0.0)
    return h, h_prev, h_next


def _ada_kernel(cv_ref, w_ref, b_ref, o_ref):
    cv = cv_ref[...]
    o_ref[...] = _dot_f32(cv * _sigmoid(cv), w_ref[...]) + b_ref[...]


def _ada_call(cv, w_ada, b_ada):
    rows, d = cv.shape
    n = w_ada.shape[1]
    bn = n // 4
    return pl.pallas_call(
        _ada_kernel,
        out_shape=jax.ShapeDtypeStruct((rows, n), F32),
        grid=(n // bn,),
        in_specs=[pl.BlockSpec((rows, d), lambda i: (0, 0)),
                  pl.BlockSpec((d, bn), lambda i: (0, i)),
                  pl.BlockSpec((1, bn), lambda i: (0, i))],
        out_specs=pl.BlockSpec((rows, bn), lambda i: (0, i)),
        compiler_params=pltpu.CompilerParams(dimension_semantics=("arbitrary",),
                                             vmem_limit_bytes=VMEM_LIMIT),
        name="ada_mod",
    )(cv, w_ada, b_ada)


def _pre_kernel(x_ref, xp_ref, xn_ref, mod_ref, g1_ref, w_rkv_ref, mu_rkv_ref, mu_wag_ref,
                wd1_ref, wd2_ref, wd0_ref, a1_ref, a2_ref, a0_ref, kk_w_ref, ka_ref, rk_ref,
                seg_ref,
                lw0_ref, lw1_ref, r_ref, kd0_ref, kd1_ref, v_ref, kk_ref, b0_ref, b1_ref,
                bonus_ref):
    d = x_ref.shape[-1]
    mod = mod_ref[...]
    shift, scale = mod[:, 0:d], mod[:, d:2 * d]
    h, h_prev, h_next = _tile_h(x_ref, xp_ref, xn_ref, g1_ref[...], scale, shift)
    h_up, h_dn = _shift_rows(h, h_prev, h_next)
    hd = 0.5 * (h_up + h_dn) - h

    w_rkv = w_rkv_ref[...]
    z = _dot(h.astype(BF16), w_rkv)
    halo = jnp.concatenate([h_prev, h_next, jnp.zeros((SUBLANES - 2, d), F32)], axis=0)
    zh = _dot(halo.astype(BF16), w_rkv)
    z_up, z_dn = _shift_rows(z, zh[0:1, :], zh[1:2, :])
    z = z + (0.5 * (z_up + z_dn) - z) * mu_rkv_ref[...]
    r, k, v = z[:, 0:d], z[:, d:2 * d], z[:, 2 * d:3 * d]

    mu = mu_wag_ref[...]
    xw = h + hd * mu[0:1, :]
    xa = h + hd * mu[1:2, :]

    tw = jnp.tanh(_dot(xw.astype(BF16), wd1_ref[...]))
    pre_w = wd0_ref[...] + _dot(tw.astype(BF16), wd2_ref[...])
    lw = -_sigmoid(pre_w) * float(np.exp(-0.5))
    ta = _dot(xa.astype(BF16), a1_ref[...])
    a = _sigmoid(a0_ref[...] + _dot(ta.astype(BF16), a2_ref[...]))

    seg = seg_ref[...]
    kkr = k * kk_w_ref[...]
    kk = kkr * lax.rsqrt(_seg_sum(kkr * kkr, seg) + NORM_EPS)
    ka = ka_ref[...]

    _store_pairs(lw0_ref, lw[:, 0:d])
    _store_pairs(lw1_ref, lw[:, d:2 * d])
    _store_pairs(r_ref, r)
    _store_pairs(v_ref, v)
    _store_pairs(kk_ref, kk)
    a_f, a_b = a[:, 0:d], a[:, d:2 * d]
    _store_pairs(kd0_ref, k * (1.0 + (a_f - 1.0) * ka))
    _store_pairs(kd1_ref, k * (1.0 + (a_b - 1.0) * ka))
    _store_pairs(b0_ref, kk * a_f)
    _store_pairs(b1_ref, kk * a_b)
    bonus_ref[...] = _seg_sum(r * k * rk_ref[...], seg) * v


def _store_pairs(ref, val):
    for q in range(ref.shape[0]):
        ref[q] = val[:, q * PAIR:(q + 1) * PAIR].astype(ref.dtype)


def _pair_spec(tm, n_pairs):
    return pl.BlockSpec((None, n_pairs, tm, PAIR), lambda b, j: (b, 0, j, 0))


def _const_spec(shape):
    nd = len(shape)
    return pl.BlockSpec(shape, lambda b, j: (0,) * nd, pipeline_mode=pl.Buffered(1))


def _token_specs(tm, seq, d):
    tiles8 = tm // SUBLANES
    last8 = seq // SUBLANES - 1
    x_spec = pl.BlockSpec((None, tm, d), lambda b, j: (b, j, 0))
    xp_spec = pl.BlockSpec((None, SUBLANES, d), lambda b, j: (b, jnp.maximum(j * tiles8 - 1, 0), 0))
    xn_spec = pl.BlockSpec((None, SUBLANES, d), lambda b, j: (b, jnp.minimum((j + 1) * tiles8, last8), 0))
    return x_spec, xp_spec, xn_spec


def _pre_call(x, mod, p, consts):
    bsz, seq, d = x.shape
    tm = TOKEN_TILE
    x_spec, xp_spec, xn_spec = _token_specs(tm, seq, d)
    mod_spec = pl.BlockSpec((None, 1, mod.shape[-1]), lambda b, j: (b % mod.shape[0], 0, 0))
    weights = (p["g1"], p["w_rkv"], p["mu_rkv"], p["mu_wag"], p["wd1"], p["wd2"], p["wd0"],
               p["a1"], p["a2"], p["a0"], p["k_k"], p["k_a"], p["r_k"], consts["seg"])
    n_pairs = d // PAIR
    pm = lambda dt: jax.ShapeDtypeStruct((bsz, n_pairs, seq, PAIR), dt)
    return pl.pallas_call(
        _pre_kernel,
        out_shape=(pm(F32), pm(F32)) + (pm(BF16),) * 7 + (jax.ShapeDtypeStruct((bsz, seq, d), F32),),
        grid=(bsz, seq // tm),
        in_specs=[x_spec, xp_spec, xn_spec, mod_spec] + [_const_spec(w.shape) for w in weights],
        out_specs=(_pair_spec(tm, n_pairs),) * 9 + (x_spec,),
        compiler_params=pltpu.CompilerParams(dimension_semantics=("parallel", "parallel"),
                                             vmem_limit_bytes=VMEM_LIMIT),
        name="pre_scan",
    )(x, x, x, mod, *weights)


def _stack(x, lane_head):
    return jnp.concatenate([jnp.where(lane_head == 0, x, 0.0), jnp.where(lane_head == 1, x, 0.0)], axis=0)


def _split2(x):
    hi = x.astype(BF16)
    return hi, (x - hi.astype(F32)).astype(BF16)


def _chunk_prep(insts, cst):
    lane_head, bd = cst["lane_head"], cst["bd"]
    c = insts[0][0].shape[0]
    ids = range(len(insts))
    rev = [t[6] for t in insts]
    stk = lambda x: _stack(x, lane_head)
    cum = [_dot_exact_lhs(cst["tri"][t[6]], t[0]) for t in insts]
    tot = [cum[i][(0 if rev[i] else c - 1):(1 if rev[i] else c), :] for i in ids]
    op = []
    for i in ids:
        lw, r, kd, v, kk, b = (t.astype(F32) for t in insts[i][:6])
        e_in = jnp.exp(cum[i])
        e_ex = jnp.exp(cum[i] - lw)
        e_ng = jnp.exp(-cum[i])
        e_end = jnp.exp(tot[i] - cum[i])
        a = kk * e_ex
        op.append(dict(
            r=r * e_in,
            ar=jnp.concatenate([a, r * e_in], axis=0).astype(BF16),
            bk=jnp.concatenate([stk(b * e_ng), stk(kd * e_ng)], axis=0).astype(BF16),
            a_st=stk(a).astype(BF16), v_st=stk(v).astype(BF16), v=v.astype(BF16),
            bh=(-(b * e_end)).astype(BF16), bkh=jnp.concatenate([-(b * e_end), kd * e_end], axis=0).astype(BF16)))
    sc = [_dot_nt(op[i]["ar"], op[i]["bk"]) for i in ids]
    strict = [cst["strict"][rev[i]] for i in ids]
    incl = [cst["incl"][rev[i]] for i in ids]
    l_w = [jnp.where(strict[i], -sc[i][:c, :PAIR], 0.0) for i in ids]
    m_kk = [jnp.concatenate([jnp.where(strict[i], sc[i][:c, PAIR:], 0.0),
                             jnp.where(incl[i], sc[i][c:, PAIR:], 0.0)], axis=0).astype(BF16) for i in ids]
    m_rb = [jnp.where(incl[i], -sc[i][c:, :PAIR], 0.0).astype(BF16) for i in ids]
    mvk = [_dot(m_kk[i], op[i]["v_st"]) for i in ids]

    x_w = [cst["eye_w"] + l_w[i] for i in ids]
    l_sp = [_split2(l_w[i]) for i in ids]
    p_w = [_dot(l_sp[i][0], stk(l_w[i]).astype(BF16)) for i in ids]
    n_sq = int(np.log2(c)) - 1
    for q in range(n_sq):
        p_bd = [stk(p_w[i]).astype(BF16) for i in ids]
        if q < n_sq - 1:
            zz = [_dot(jnp.concatenate([x_w[i], p_w[i]], axis=0).astype(BF16), p_bd[i]) for i in ids]
            x_w = [x_w[i] + zz[i][:c] for i in ids]
            p_w = [zz[i][c:] for i in ids]
        else:
            x_w = [x_w[i] + _dot(x_w[i].astype(BF16), p_bd[i]) for i in ids]
    t_sp = [_split2(stk(x_w[i])) for i in ids]
    lt = [_dot(jnp.concatenate(l_sp[i], axis=0), t_sp[i][0]) for i in ids]
    lt2 = [_dot(l_sp[i][0], t_sp[i][1]) for i in ids]
    e_w = [(cst["eye_w"] - x_w[i]) + (lt[i][:c] + lt[i][c:] + lt2[i]) for i in ids]
    x_w = [x_w[i] + _dot(x_w[i].astype(BF16), stk(e_w[i]).astype(BF16)) for i in ids]

    w_p = [_dot(x_w[i].astype(BF16), jnp.concatenate([op[i]["a_st"], stk(mvk[i][:c]).astype(BF16)], axis=1))
           for i in ids]
    w_st = [jnp.concatenate([stk(w_p[i][:, :PAIR]), stk(w_p[i][:, PAIR:])], axis=1).astype(BF16) for i in ids]
    ry = [_dot(m_rb[i], w_st[i]) for i in ids]
    g_f = [_dot_tn(op[i]["bh"], w_p[i][:, :PAIR].astype(BF16)) for i in ids]
    uv = [jnp.concatenate([w_p[i][:, PAIR:].astype(BF16), op[i]["v"]], axis=0) for i in ids]
    h_f = [_dot_tn(op[i]["bkh"], uv[i]) for i in ids]
    out = []
    for i in ids:
        r_p = op[i]["r"] + ry[i][:, :PAIR]
        g_m = jnp.where(cst["eye"], jnp.exp(tot[i]), 0.0) + jnp.where(bd, g_f[i], 0.0)
        out.append((jnp.concatenate([r_p, g_m], axis=0).astype(BF16), ry[i][:, PAIR:] + mvk[i][c:],
                    jnp.where(bd, h_f[i], 0.0)))
    return out


def _scan_consts(c):
    n = 2 * c
    row = lax.broadcasted_iota(jnp.int32, (n, n), 0)
    col = lax.broadcasted_iota(jnp.int32, (n, n), 1)
    tr = lax.broadcasted_iota(jnp.int32, (c, c), 0)
    tc = lax.broadcasted_iota(jnp.int32, (c, c), 1)
    t = lax.broadcasted_iota(jnp.int32, (c, PAIR), 0)
    lane = lax.broadcasted_iota(jnp.int32, (c, PAIR), 1)
    s = lane % HEAD_DIM
    return {
        "tri": (jnp.where(tc <= tr, 1.0, 0.0).astype(BF16), jnp.where(tc >= tr, 1.0, 0.0).astype(BF16)),
        "strict": (s < t, s > t),
        "incl": (s <= t, s >= t),
        "eye_w": jnp.where(s == t, 1.0, 0.0),
        "eye": row == col,
        "bd": (row // c) == (col // c),
        "lane_head": lane // HEAD_DIM,
    }


def _scan_kernel(*refs, has_s0, want_final):
    lw0_ref, lw1_ref, r_ref, kd0_ref, kd1_ref, v_ref, kk_ref, b0_ref, b1_ref = refs[:9]
    pos = 9
    s0_ref = None
    if has_s0:
        s0_ref = refs[pos]
        pos += 1
    y_ref = refs[pos]
    pos += 1
    sf_ref = None
    if want_final:
        sf_ref = refs[pos]
        pos += 1
    yb_ref, st_ref, rg_ref, h_ref = refs[pos:pos + 4]

    c = CHUNK
    pp, seq = r_ref.shape[0], r_ref.shape[1]
    nc = seq // c
    ipt = min(ITEMS_PER_TRIP, pp * nc)
    cst = _scan_consts(c)
    zero = jnp.zeros((HEAD_DIM, HEAD_DIM), F32)
    lw_refs, kd_refs, b_refs, yv_refs = (lw0_ref, lw1_ref), (kd0_ref, kd1_ref), (b0_ref, b1_ref), (y_ref, yb_ref)

    for q in range(pp):
        for dr in range(2):
            if has_s0:
                s_a = s0_ref[dr, 2 * q].T
                s_b = s0_ref[dr, 2 * q + 1].T
                st_ref[q, dr] = jnp.concatenate([jnp.concatenate([s_a, zero], axis=1),
                                                 jnp.concatenate([zero, s_b], axis=1)], axis=0)
            else:
                st_ref[q, dr] = jnp.zeros((PAIR, PAIR), F32)

    def prep_body(t, carry):
        insts, dest = [], []
        for j in range(ipt):
            item = t * ipt + j
            q, ck = item // nc, item % nc
            rows = pl.ds(pl.multiple_of(ck * c, c), c)
            r, v, kk = r_ref[q, rows, :], v_ref[q, rows, :], kk_ref[q, rows, :]
            for dr in range(2):
                insts.append((lw_refs[dr][q, rows, :], r, kd_refs[dr][q, rows, :], v, kk, b_refs[dr][q, rows, :], dr))
                dest.append((q, dr, ck, rows))
        for (q, dr, ck, rows), (rg, y_v, h_m) in zip(dest, _chunk_prep(insts, cst)):
            rg_ref[q, dr, ck] = rg
            h_ref[q, dr, ck] = h_m
            yv_refs[dr][q, rows, :] = y_v
        return carry

    lax.fori_loop(0, (pp * nc) // ipt, prep_body, 0)

    def chain_body(i, carry):
        steps = [(q, dr, ck) for q in range(pp) for dr, ck in ((0, i), (1, nc - 1 - i))]
        z = [_dot(rg_ref[q, dr, ck], st_ref[q, dr].astype(BF16)) for q, dr, ck in steps]
        for (q, dr, ck), z_i in zip(steps, z):
            rows = pl.ds(pl.multiple_of(ck * c, c), c)
            yv_refs[dr][q, rows, :] = yv_refs[dr][q, rows, :] + z_i[:c]
            st_ref[q, dr] = z_i[c:] + h_ref[q, dr, ck]
        return carry

    lax.fori_loop(0, nc, chain_body, 0)
    y_ref[...] = y_ref[...] + yb_ref[...]
    if want_final:
        for q in range(pp):
            for dr in range(2):
                s_bd = st_ref[q, dr]
                sf_ref[dr, 2 * q] = s_bd[:HEAD_DIM, :HEAD_DIM].T
                sf_ref[dr, 2 * q + 1] = s_bd[HEAD_DIM:, HEAD_DIM:].T


def _pairs_per_step(n_pairs, seq, ops):
    nc = seq // CHUNK
    io = (sum(o.dtype.itemsize for o in ops) + 4) * 2 * seq * PAIR
    scratch = 4 * seq * PAIR + 2 * nc * ((CHUNK + PAIR) * PAIR * 2 + PAIR * PAIR * 4) + 2 * PAIR * PAIR * 4
    pp = 1
    while pp * 2 <= n_pairs and n_pairs % (pp * 2) == 0 and (pp * 2) * (io + scratch) <= SCAN_VMEM_BUDGET:
        pp *= 2
    return pp


def _scan_call(ops, s0, want_final):
    bsz, n_pairs, seq, _ = ops[0].shape
    nc = seq // CHUNK
    pp = _pairs_per_step(n_pairs, seq, ops)
    assert (pp * nc) % min(ITEMS_PER_TRIP, pp * nc) == 0
    col_spec = pl.BlockSpec((None, pp, seq, PAIR), lambda b, g: (b, g, 0, 0))
    st_spec = pl.BlockSpec((None, 2, 2 * pp, HEAD_DIM, HEAD_DIM), lambda b, g: (b, 0, g, 0, 0))
    in_specs = [col_spec] * 9
    args = list(ops)
    has_s0 = s0 is not None
    if has_s0:
        in_specs.append(st_spec)
        args.append(s0)
    out_shape = [jax.ShapeDtypeStruct((bsz, n_pairs, seq, PAIR), F32)]
    out_specs = [col_spec]
    if want_final:
        out_shape.append(jax.ShapeDtypeStruct((bsz, 2, N_HEADS, HEAD_DIM, HEAD_DIM), F32))
        out_specs.append(st_spec)
    res = pl.pallas_call(
        functools.partial(_scan_kernel, has_s0=has_s0, want_final=want_final),
        out_shape=tuple(out_shape),
        grid=(bsz, n_pairs // pp),
        in_specs=in_specs,
        out_specs=tuple(out_specs),
        scratch_shapes=[pltpu.VMEM((pp, seq, PAIR), F32), pltpu.VMEM((pp, 2, PAIR, PAIR), F32),
                        pltpu.VMEM((pp, 2, nc, CHUNK + PAIR, PAIR), BF16), pltpu.VMEM((pp, 2, nc, PAIR, PAIR), F32)],
        compiler_params=pltpu.CompilerParams(dimension_semantics=("parallel", "parallel"),
                                             vmem_limit_bytes=VMEM_LIMIT),
        name="wkv_scan",
    )(*args)
    return res if want_final else (res[0], None)


def _post_kernel(x_ref, xp_ref, xn_ref, mod_ref, y_ref, bonus_ref, g1_ref, g2_ref, gf_ref,
                 w_pg_ref, mu_wag_ref, gw1_ref, gw2_ref, lnw_ref, lnb_ref, pool_w_ref, pool_scale_ref,
                 band_ref, icnt_ref, seg_ref, w_out_ref, w_ff1_ref, w_ff2_ref, o_ref, *, final_norm):
    d = x_ref.shape[-1]
    d_pool = len(POOL_WINDOWS) * POOL_GROUP
    mod = mod_ref[...]
    sh1, sc1, ga1, sh2, sc2, ga2 = (mod[:, i * d:(i + 1) * d] for i in range(N_MOD))
    x = x_ref[...]
    h, h_prev, h_next = _tile_h(x_ref, xp_ref, xn_ref, g1_ref[...], sc1, sh1)
    h_up, h_dn = _shift_rows(h, h_prev, h_next)
    xg = h + (0.5 * (h_up + h_dn) - h) * mu_wag_ref[2:3, :]

    z = _dot(h.astype(BF16), w_pg_ref[...])
    outs = []
    for g in range(len(POOL_WINDOWS)):
        zg = z[:, g * POOL_GROUP:(g + 1) * POOL_GROUP]
        zh = zg.astype(BF16)
        zl = (zg - zh.astype(F32)).astype(BF16)
        band = band_ref[g]
        mixed = (_dot(band, zh) + _dot(band, zl)) * icnt_ref[g] - zg
        outs.append(_dot(mixed.astype(BF16), pool_w_ref[g]))
    a_out = jnp.concatenate(outs, axis=-1) * pool_scale_ref[...]
    gate_a = _sigmoid(z[:, d_pool:d_pool + d])
    gate_b = _sigmoid(z[:, d_pool + d:d_pool + 2 * d])
    gl = _sigmoid(_dot(xg.astype(BF16), gw1_ref[...]))
    g_out = _dot(gl.astype(BF16), gw2_ref[...])

    seg = seg_ref[...]
    y = jnp.concatenate([y_ref[q] for q in range(y_ref.shape[0])], axis=-1)
    yc = y - _seg_sum(y, seg) * (1.0 / HEAD_DIM)
    var = _seg_sum(yc * yc, seg) * (1.0 / HEAD_DIM)
    yn = yc * lax.rsqrt(var + LNX_EPS) * lnw_ref[...] + lnb_ref[...]
    b_out = (yn + bonus_ref[...]) * g_out
    mix = _dot((gate_a * a_out + gate_b * b_out).astype(BF16), w_out_ref[...])
    x = x + ga1 * mix

    h2 = _modulated_norm(x, g2_ref[...], sc2, sh2)
    u = jnp.maximum(_dot(h2.astype(BF16), w_ff1_ref[...]), 0.0)
    ff = _dot((u * u).astype(BF16), w_ff2_ref[...])
    x = x + ga2 * ff
    if final_norm:
        x = x * lax.rsqrt(jnp.mean(x * x, axis=-1, keepdims=True) + RMS_EPS) * gf_ref[...]
    o_ref[...] = x


def _post_call(x, mod, y, bonus, p, consts, g_final, final_norm):
    bsz, seq, d = x.shape
    tm = TOKEN_TILE
    x_spec, xp_spec, xn_spec = _token_specs(tm, seq, d)
    mod_spec = pl.BlockSpec((None, 1, mod.shape[-1]), lambda b, j: (b % mod.shape[0], 0, 0))
    weights = (p["g1"], p["g2"], g_final, p["w_pg"], p["mu_wag"], p["gw1"], p["gw2"], p["ln_x_w"],
               p["ln_x_b"], p["pool_w"], p["pool_scale"], consts["band"], consts["icnt"], consts["seg"],
               p["w_out"], p["w_ff1"], p["w_ff2"])
    return pl.pallas_call(
        functools.partial(_post_kernel, final_norm=final_norm),
        out_shape=jax.ShapeDtypeStruct((bsz, seq, d), F32),
        grid=(bsz, seq // tm),
        in_specs=[x_spec, xp_spec, xn_spec, mod_spec, _pair_spec(tm, d // PAIR), x_spec]
        + [_const_spec(w.shape) for w in weights],
        out_specs=x_spec,
        compiler_params=pltpu.CompilerParams(dimension_semantics=("parallel", "parallel"),
                                             vmem_limit_bytes=VMEM_LIMIT),
        name="post_scan",
    )(x, x, x, mod, y, bonus, *weights)


def _pool_consts(tm, row_len):
    t = np.arange(tm)
    pos, row = t % row_len, t // row_len
    bands, icnts = [], []
    for win in POOL_WINDOWS:
        lo = np.clip(pos - win // 2, 0, row_len)
        hi = np.clip(pos + win - win // 2, 0, row_len)
        inside = (pos[None, :] >= lo[:, None]) & (pos[None, :] < hi[:, None]) & (row[None, :] == row[:, None])
        bands.append(inside.astype(np.float32))
        icnts.append(np.broadcast_to((1.0 / (hi - lo))[:, None], (tm, POOL_GROUP)).astype(np.float32))
    return jnp.asarray(np.stack(bands), BF16), jnp.asarray(np.stack(icnts), F32)


def _block_diag2(w):
    z = jnp.zeros_like(w[0])
    return jnp.concatenate([jnp.concatenate([w[0], z], axis=1), jnp.concatenate([z, w[1]], axis=1)], axis=0)


def _layer_params(li, w):
    d = w["w_in"].shape[1]
    d_pool = len(POOL_WINDOWS) * POOL_GROUP
    w_in = w["w_in"][li].astype(BF16)
    row = lambda a: a.reshape(1, -1)
    return {
        "g1": row(w["g_norm1"][li]), "g2": row(w["g_norm2"][li]),
        "w_rkv": w_in[:, d_pool:d_pool + 3 * d],
        "w_pg": jnp.concatenate([w_in[:, :d_pool], w_in[:, d_pool + 3 * d:]], axis=1),
        "mu_rkv": row(w["mu_rkv"][li]), "mu_wag": w["mu_wag"][li],
        "wd1": jnp.concatenate([w["w_dec1"][li, 0], w["w_dec1"][li, 1]], axis=1).astype(BF16),
        "wd2": _block_diag2(w["w_dec2"][li]).astype(BF16), "wd0": row(w["w_dec0"][li]),
        "a1": jnp.concatenate([w["a1"][li, 0], w["a1"][li, 1]], axis=1).astype(BF16),
        "a2": _block_diag2(w["a2"][li]).astype(BF16), "a0": row(w["a0"][li]),
        "gw1": w["gate_w1"][li].astype(BF16), "gw2": w["gate_w2"][li].astype(BF16),
        "k_k": row(w["k_k"][li]), "k_a": row(w["k_a"][li]), "r_k": row(w["r_k"][li]),
        "ln_x_w": row(w["ln_x_w"][li]), "ln_x_b": row(w["ln_x_b"][li]),
        "pool_w": w["pool_w"][li].astype(BF16), "pool_scale": row(w["pool_scale"][li]),
        "w_out": w["w_out"][li].astype(BF16), "w_ff1": w["w_ff1"][li].astype(BF16),
        "w_ff2": w["w_ff2"][li].astype(BF16),
    }


def _layer(x, mod, s0, want_final, p, consts, g_final, final_norm):
    ops = _pre_call(x, mod, p, consts)
    y, s_fin = _scan_call(ops[:9], s0, want_final)
    return _post_call(x, mod, y, ops[9], p, consts, g_final, final_norm), s_fin


def kernel(x_prompt, x_sample, state_rwkv, c, c_ctx, w_ada, b_ada, g_norm1, g_norm2, w_in, mu_rkv, mu_wag, w_dec0, w_dec1, w_dec2, a0, a1, a2, gate_w1, gate_w2, k_k, k_a, r_k, ln_x_w, ln_x_b, pool_w, pool_scale, w_out, w_ff1, w_ff2, g_final):
    w = dict(w_ada=w_ada, b_ada=b_ada, g_norm1=g_norm1, g_norm2=g_norm2, w_in=w_in, mu_rkv=mu_rkv,
             mu_wag=mu_wag, w_dec0=w_dec0, w_dec1=w_dec1, w_dec2=w_dec2, a0=a0, a1=a1, a2=a2,
             gate_w1=gate_w1, gate_w2=gate_w2, k_k=k_k, k_a=k_a, r_k=r_k, ln_x_w=ln_x_w, ln_x_b=ln_x_b,
             pool_w=pool_w, pool_scale=pool_scale, w_out=w_out, w_ff1=w_ff1, w_ff2=w_ff2)
    depth = w_in.shape[0]
    n_lat, lat_len, d = x_sample.shape
    ctx_len = x_prompt.shape[1]
    assert ctx_len % TOKEN_TILE == 0 and lat_len % TOKEN_TILE == 0 and TOKEN_TILE % GRID_W == 0
    assert ctx_len == TOKEN_TILE, "context pooling runs over the whole sequence inside one token tile"

    seg = jnp.asarray(np.kron(np.eye(SEG_TILE // HEAD_DIM), np.ones((HEAD_DIM, HEAD_DIM))), BF16)
    band_c, icnt_c = _pool_consts(TOKEN_TILE, ctx_len)
    band_l, icnt_l = _pool_consts(TOKEN_TILE, GRID_W)
    consts_ctx = {"seg": seg, "band": band_c, "icnt": icnt_c}
    consts_lat = {"seg": seg, "band": band_l, "icnt": icnt_l}

    n_cv = 1 + n_lat
    cv = jnp.concatenate([c_ctx[None, :], c, jnp.zeros((-n_cv % SUBLANES, d), F32)], axis=0)
    g_fin = g_final.reshape(1, d)
    ctx, lat = x_prompt, x_sample
    new_states = []
    for li in range(depth):
        p = _layer_params(li, w)
        mod = _ada_call(cv, w_ada[li], b_ada[li].reshape(1, -1))
        mod_ctx = mod[0:1].reshape(1, 1, -1)
        mod_lat = mod[1:n_cv].reshape(n_lat, 1, -1)
        last = li == depth - 1
        ctx, s_fin = _layer(ctx, mod_ctx, None, True, p, consts_ctx, g_fin, last)
        new_states.append(s_fin)
        lat, _ = _layer(lat, mod_lat, state_rwkv[:, li], False, p, consts_lat, g_fin, last)
    return ctx, lat, jnp.stack(new_states, axis=1)
```

```python
import functools

import numpy as np
import jax
import jax.numpy as jnp
from jax import lax
from jax.experimental import pallas as pl
from jax.experimental.pallas import tpu as pltpu

F32 = jnp.float32
BF16 = jnp.bfloat16

N_HEADS = 16
HEAD_DIM = 64
PAIR = 2 * HEAD_DIM
POOL_WINDOWS = (2, 4, 8, 16)
POOL_GROUP = 128
GRID_W = 64
N_MOD = 6
RMS_EPS = 1e-6
LNX_EPS = 64e-5
NORM_EPS = 1e-12

CHUNK = 64
ITEMS_PER_TRIP = 8
SCAN_VMEM_BUDGET = 46 * 1024 * 1024
TOKEN_TILE = 256
SEG_TILE = 256
SUBLANES = 8
VMEM_LIMIT = 56 * 1024 * 1024


def _dot(a, b):
    return jnp.dot(a, b, preferred_element_type=F32)


def _dot_nt(a, b):
    return lax.dot_general(a, b, (((1,), (1,)), ((), ())), preferred_element_type=F32)


def _dot_tn(a, b):
    return lax.dot_general(a, b, (((0,), (0,)), ((), ())), preferred_element_type=F32)


def _dot_f32(a, b):
    return jnp.dot(a, b, precision=lax.Precision.HIGHEST, preferred_element_type=F32)


def _split2(x):
    hi = x.astype(BF16)
    return hi, (x - hi.astype(F32)).astype(BF16)


def _dot_exact_rhs(x, m):
    hi, lo = _split2(x)
    return _dot(hi, m) + _dot(lo, m)


def _dot_exact_lhs(m, x):
    hi, lo = _split2(x)
    return _dot(m, hi) + _dot(m, lo)


def _seg_sum(x, seg_ones):
    d = x.shape[-1]
    parts = [_dot_exact_rhs(x[:, q:q + SEG_TILE], seg_ones) for q in range(0, d, SEG_TILE)]
    return jnp.concatenate(parts, axis=-1)


def _sigmoid(x):
    return 1.0 / (1.0 + jnp.exp(-x))


def _modulated_norm(x, g, scale, shift):
    y = x * lax.rsqrt(jnp.mean(x * x, axis=-1, keepdims=True) + RMS_EPS)
    return (y * g) * (1.0 + scale) + shift


def _shift_rows(v, first_row, last_row):
    n = v.shape[0]
    row = lax.broadcasted_iota(jnp.int32, v.shape, 0)
    up = jnp.where(row == 0, first_row, pltpu.roll(v, 1, 0))
    dn = jnp.where(row == n - 1, last_row, pltpu.roll(v, n - 1, 0))
    return up, dn


def _tile_h(x_ref, xp_ref, xn_ref, g, scale, shift):
    j = pl.program_id(1)
    nj = pl.num_programs(1)
    h = _modulated_norm(x_ref[...], g, scale, shift)
    halo = jnp.concatenate([xp_ref[SUBLANES - 1:SUBLANES, :], xn_ref[0:1, :]], axis=0)
    hh = _modulated_norm(halo, g, scale, shift)
    h_prev = jnp.where(j > 0, hh[0:1, :], 0.0)
    h_next = jnp.where(j < nj - 1, hh[1:2, :], 0.0)
    return h, h_prev, h_next


def _ada_kernel(cv_ref, w_ref, b_ref, o_ref):
    cv = cv_ref[...]
    o_ref[...] = _dot_f32(cv * _sigmoid(cv), w_ref[...]) + b_ref[...]


def _ada_call(cv, w_ada, b_ada):
    rows, d = cv.shape
    n = w_ada.shape[1]
    bn = n // 4
    return pl.pallas_call(
        _ada_kernel,
        out_shape=jax.ShapeDtypeStruct((rows, n), F32),
        grid=(n // bn,),
        in_specs=[pl.BlockSpec((rows, d), lambda i: (0, 0)),
                  pl.BlockSpec((d, bn), lambda i: (0, i)),
                  pl.BlockSpec((1, bn), lambda i: (0, i))],
        out_specs=pl.BlockSpec((rows, bn), lambda i: (0, i)),
        compiler_params=pltpu.CompilerParams(dimension_semantics=("arbitrary",),
                                             vmem_limit_bytes=VMEM_LIMIT),
        name="ada_mod",
    )(cv, w_ada, b_ada)


def _pre_kernel(x_ref, xp_ref, xn_ref, mod_ref, g1_ref, w_rkv_ref, mu_rkv_ref, mu_wag_ref,
                wd1_ref, wd2_ref, wd0_ref, a1_ref, a2_ref, a0_ref, kk_w_ref, ka_ref, rk_ref,
                seg_ref,
                lw0_ref, lw1_ref, r_ref, kd0_ref, kd1_ref, v_ref, kk_ref, b0_ref, b1_ref,
                bonus_ref):
    d = x_ref.shape[-1]
    mod = mod_ref[...]
    shift, scale = mod[:, 0:d], mod[:, d:2 * d]
    h, h_prev, h_next = _tile_h(x_ref, xp_ref, xn_ref, g1_ref[...], scale, shift)
    h_up, h_dn = _shift_rows(h, h_prev, h_next)
    hd = 0.5 * (h_up + h_dn) - h

    w_rkv = w_rkv_ref[...]
    z = _dot(h.astype(BF16), w_rkv)
    halo = jnp.concatenate([h_prev, h_next, jnp.zeros((SUBLANES - 2, d), F32)], axis=0)
    zh = _dot(halo.astype(BF16), w_rkv)
    z_up, z_dn = _shift_rows(z, zh[0:1, :], zh[1:2, :])
    mu_rkv = mu_rkv_ref[...]
    z = z * (1.0 - mu_rkv) + (z_up + z_dn) * (0.5 * mu_rkv)
    r, k, v = z[:, 0:d], z[:, d:2 * d], z[:, 2 * d:3 * d]

    mu = mu_wag_ref[...]
    xw = h + hd * mu[0:1, :]
    xa = h + hd * mu[1:2, :]

    tw = jnp.tanh(_dot(xw.astype(BF16), wd1_ref[...]))
    pre_w = wd0_ref[...] + _dot(tw.astype(BF16), wd2_ref[...])
    lw = -_sigmoid(pre_w) * float(np.exp(-0.5))
    ta = _dot(xa.astype(BF16), a1_ref[...])
    a = _sigmoid(a0_ref[...] + _dot(ta.astype(BF16), a2_ref[...]))

    seg = seg_ref[...]
    kkr = k * kk_w_ref[...]
    kk = kkr * lax.rsqrt(_seg_sum(kkr * kkr, seg) + NORM_EPS)
    ka = ka_ref[...]

    _store_pairs(lw0_ref, lw[:, 0:d])
    _store_pairs(lw1_ref, lw[:, d:2 * d])
    _store_pairs(r_ref, r)
    _store_pairs(v_ref, v)
    _store_pairs(kk_ref, kk)
    a_f, a_b = a[:, 0:d], a[:, d:2 * d]
    _store_pairs(kd0_ref, k * (1.0 + (a_f - 1.0) * ka))
    _store_pairs(kd1_ref, k * (1.0 + (a_b - 1.0) * ka))
    _store_pairs(b0_ref, kk * a_f)
    _store_pairs(b1_ref, kk * a_b)
    bonus_ref[...] = _seg_sum(r * k * rk_ref[...], seg) * v


def _store_pairs(ref, val):
    for q in range(ref.shape[0]):
        ref[q] = val[:, q * PAIR:(q + 1) * PAIR].astype(ref.dtype)


def _pair_spec(tm, n_pairs):
    return pl.BlockSpec((None, n_pairs, tm, PAIR), lambda b, j: (b, 0, j, 0))


def _const_spec(shape):
    nd = len(shape)
    return pl.BlockSpec(shape, lambda b, j: (0,) * nd, pipeline_mode=pl.Buffered(1))


def _token_specs(tm, seq, d):
    tiles8 = tm // SUBLANES
    last8 = seq // SUBLANES - 1
    x_spec = pl.BlockSpec((None, tm, d), lambda b, j: (b, j, 0))
    xp_spec = pl.BlockSpec((None, SUBLANES, d), lambda b, j: (b, jnp.maximum(j * tiles8 - 1, 0), 0))
    xn_spec = pl.BlockSpec((None, SUBLANES, d), lambda b, j: (b, jnp.minimum((j + 1) * tiles8, last8), 0))
    return x_spec, xp_spec, xn_spec


def _pre_call(x, mod, p, consts):
    bsz, seq, d = x.shape
    tm = TOKEN_TILE
    x_spec, xp_spec, xn_spec = _token_specs(tm, seq, d)
    mod_spec = pl.BlockSpec((None, 1, mod.shape[-1]), lambda b, j: (b % mod.shape[0], 0, 0))
    weights = (p["g1"], p["w_rkv"], p["mu_rkv"], p["mu_wag"], p["wd1"], p["wd2"], p["wd0"],
               p["a1"], p["a2"], p["a0"], p["k_k"], p["k_a"], p["r_k"], consts["seg"])
    n_pairs = d // PAIR
    pm = lambda dt: jax.ShapeDtypeStruct((bsz, n_pairs, seq, PAIR), dt)
    return pl.pallas_call(
        _pre_kernel,
        out_shape=(pm(F32), pm(F32)) + (pm(BF16),) * 7 + (jax.ShapeDtypeStruct((bsz, seq, d), F32),),
        grid=(bsz, seq // tm),
        in_specs=[x_spec, xp_spec, xn_spec, mod_spec] + [_const_spec(w.shape) for w in weights],
        out_specs=(_pair_spec(tm, n_pairs),) * 9 + (x_spec,),
        compiler_params=pltpu.CompilerParams(dimension_semantics=("parallel", "parallel"),
                                             vmem_limit_bytes=VMEM_LIMIT),
        name="pre_scan",
    )(x, x, x, mod, *weights)


def _stack(x, lane_head):
    return jnp.concatenate([jnp.where(lane_head == 0, x, 0.0), jnp.where(lane_head == 1, x, 0.0)], axis=0)


def _chunk_prep(insts, cst):
    lane_head, bd = cst["lane_head"], cst["bd"]
    c = insts[0][0].shape[0]
    ids = range(len(insts))
    rev = [t[6] for t in insts]
    stk = lambda x: _stack(x, lane_head)
    cum = [_dot_exact_lhs(cst["tri"][t[6]], t[0]) for t in insts]
    tot = [cum[i][(0 if rev[i] else c - 1):(1 if rev[i] else c), :] for i in ids]
    op = []
    for i in ids:
        lw, r, kd, v, kk, b = (t.astype(F32) for t in insts[i][:6])
        e_in = jnp.exp(cum[i])
        e_ex = jnp.exp(cum[i] - lw)
        e_ng = jnp.exp(-cum[i])
        e_end = jnp.exp(tot[i] - cum[i])
        a = kk * e_ex
        op.append(dict(
            r=r * e_in,
            ar=jnp.concatenate([a, r * e_in], axis=0).astype(BF16),
            bk=jnp.concatenate([stk(b * e_ng), stk(kd * e_ng)], axis=0).astype(BF16),
            a_st=stk(a).astype(BF16), v_st=stk(v).astype(BF16), v=v.astype(BF16),
            bh=(-(b * e_end)).astype(BF16), bkh=jnp.concatenate([-(b * e_end), kd * e_end], axis=0).astype(BF16)))
    sc = [_dot_nt(op[i]["ar"], op[i]["bk"]) for i in ids]
    strict = [cst["strict"][rev[i]] for i in ids]
    incl = [cst["incl"][rev[i]] for i in ids]
    l_w = [jnp.where(strict[i], -sc[i][:c, :PAIR], 0.0) for i in ids]
    m_kk = [jnp.concatenate([jnp.where(strict[i], sc[i][:c, PAIR:], 0.0),
                             jnp.where(incl[i], sc[i][c:, PAIR:], 0.0)], axis=0).astype(BF16) for i in ids]
    m_rb = [jnp.where(incl[i], -sc[i][c:, :PAIR], 0.0).astype(BF16) for i in ids]
    mvk = [_dot(m_kk[i], op[i]["v_st"]) for i in ids]

    x_w = [cst["eye_w"] + l_w[i] for i in ids]
    l_sp = [_split2(l_w[i]) for i in ids]
    p_w = [_dot(l_sp[i][0], stk(l_w[i]).astype(BF16)) for i in ids]
    n_sq = int(np.log2(c)) - 1
    for q in range(n_sq):
        p_bd = [stk(p_w[i]).astype(BF16) for i in ids]
        if q < n_sq - 1:
            zz = [_dot(jnp.concatenate([x_w[i], p_w[i]], axis=0).astype(BF16), p_bd[i]) for i in ids]
            x_w = [x_w[i] + zz[i][:c] for i in ids]
            p_w = [zz[i][c:] for i in ids]
        else:
            x_w = [x_w[i] + _dot(x_w[i].astype(BF16), p_bd[i]) for i in ids]
    t_sp = [_split2(stk(x_w[i])) for i in ids]
    lt = [_dot(jnp.concatenate(l_sp[i], axis=0), t_sp[i][0]) for i in ids]
    lt2 = [_dot(l_sp[i][0], t_sp[i][1]) for i in ids]
    e_w = [(cst["eye_w"] - x_w[i]) + (lt[i][:c] + lt[i][c:] + lt2[i]) for i in ids]
    x_w = [x_w[i] + _dot(x_w[i].astype(BF16), stk(e_w[i]).astype(BF16)) for i in ids]

    w_p = [_dot(x_w[i].astype(BF16), jnp.concatenate([op[i]["a_st"], stk(mvk[i][:c]).astype(BF16)], axis=1))
           for i in ids]
    w_st = [jnp.concatenate([stk(w_p[i][:, :PAIR]), stk(w_p[i][:, PAIR:])], axis=1).astype(BF16) for i in ids]
    ry = [_dot(m_rb[i], w_st[i]) for i in ids]
    g_f = [_dot_tn(op[i]["bh"], w_p[i][:, :PAIR].astype(BF16)) for i in ids]
    uv = [jnp.concatenate([w_p[i][:, PAIR:].astype(BF16), op[i]["v"]], axis=0) for i in ids]
    h_f = [_dot_tn(op[i]["bkh"], uv[i]) for i in ids]
    out = []
    for i in ids:
        r_p = op[i]["r"] + ry[i][:, :PAIR]
        g_m = jnp.where(cst["eye"], jnp.exp(tot[i]), 0.0) + jnp.where(bd, g_f[i], 0.0)
        out.append((jnp.concatenate([r_p, g_m], axis=0).astype(BF16), ry[i][:, PAIR:] + mvk[i][c:],
                    jnp.where(bd, h_f[i], 0.0)))
    return out


def _scan_consts(c):
    n = 2 * c
    row = lax.broadcasted_iota(jnp.int32, (n, n), 0)
    col = lax.broadcasted_iota(jnp.int32, (n, n), 1)
    tr = lax.broadcasted_iota(jnp.int32, (c, c), 0)
    tc = lax.broadcasted_iota(jnp.int32, (c, c), 1)
    t = lax.broadcasted_iota(jnp.int32, (c, PAIR), 0)
    lane = lax.broadcasted_iota(jnp.int32, (c, PAIR), 1)
    s = lane % HEAD_DIM
    return {
        "tri": (jnp.where(tc <= tr, 1.0, 0.0).astype(BF16), jnp.where(tc >= tr, 1.0, 0.0).astype(BF16)),
        "strict": (s < t, s > t),
        "incl": (s <= t, s >= t),
        "eye_w": jnp.where(s == t, 1.0, 0.0),
        "eye": row == col,
        "bd": (row // c) == (col // c),
        "lane_head": lane // HEAD_DIM,
    }


def _scan_kernel(*refs, has_s0, want_final):
    lw0_ref, lw1_ref, r_ref, kd0_ref, kd1_ref, v_ref, kk_ref, b0_ref, b1_ref = refs[:9]
    pos = 9
    s0_ref = None
    if has_s0:
        s0_ref = refs[pos]
        pos += 1
    y_ref = refs[pos]
    pos += 1
    sf_ref = None
    if want_final:
        sf_ref = refs[pos]
        pos += 1
    yb_ref, st_ref, rg_ref, h_ref = refs[pos:pos + 4]

    c = CHUNK
    pp, seq = r_ref.shape[0], r_ref.shape[1]
    nc = seq // c
    ipt = min(ITEMS_PER_TRIP, pp * nc)
    cst = _scan_consts(c)
    zero = jnp.zeros((HEAD_DIM, HEAD_DIM), F32)
    lw_refs, kd_refs, b_refs, yv_refs = (lw0_ref, lw1_ref), (kd0_ref, kd1_ref), (b0_ref, b1_ref), (y_ref, yb_ref)

    for q in range(pp):
        for dr in range(2):
            if has_s0:
                s_a = s0_ref[dr, 2 * q].T
                s_b = s0_ref[dr, 2 * q + 1].T
                st_ref[q, dr] = jnp.concatenate([jnp.concatenate([s_a, zero], axis=1),
                                                 jnp.concatenate([zero, s_b], axis=1)], axis=0)
            else:
                st_ref[q, dr] = jnp.zeros((PAIR, PAIR), F32)

    def prep_body(t, carry):
        insts, dest = [], []
        for j in range(ipt):
            item = t * ipt + j
            q, ck = item // nc, item % nc
            rows = pl.ds(pl.multiple_of(ck * c, c), c)
            r, v, kk = r_ref[q, rows, :], v_ref[q, rows, :], kk_ref[q, rows, :]
            for dr in range(2):
                insts.append((lw_refs[dr][q, rows, :], r, kd_refs[dr][q, rows, :], v, kk, b_refs[dr][q, rows, :], dr))
                dest.append((q, dr, ck, rows))
        for (q, dr, ck, rows), (rg, y_v, h_m) in zip(dest, _chunk_prep(insts, cst)):
            rg_ref[q, dr, ck] = rg
            h_ref[q, dr, ck] = h_m
            yv_refs[dr][q, rows, :] = y_v
        return carry

    lax.fori_loop(0, (pp * nc) // ipt, prep_body, 0)

    def chain_body(i, carry):
        steps = [(q, dr, ck) for q in range(pp) for dr, ck in ((0, i), (1, nc - 1 - i))]
        z = [_dot(rg_ref[q, dr, ck], st_ref[q, dr].astype(BF16)) for q, dr, ck in steps]
        for (q, dr, ck), z_i in zip(steps, z):
            rows = pl.ds(pl.multiple_of(ck * c, c), c)
            yv_refs[dr][q, rows, :] = yv_refs[dr][q, rows, :] + z_i[:c]
            st_ref[q, dr] = z_i[c:] + h_ref[q, dr, ck]
        return carry

    lax.fori_loop(0, nc, chain_body, 0)
    y_ref[...] = y_ref[...] + yb_ref[...]
    if want_final:
        for q in range(pp):
            for dr in range(2):
                s_bd = st_ref[q, dr]
                sf_ref[dr, 2 * q] = s_bd[:HEAD_DIM, :HEAD_DIM].T
                sf_ref[dr, 2 * q + 1] = s_bd[HEAD_DIM:, HEAD_DIM:].T


def _pairs_per_step(n_pairs, seq, ops):
    nc = seq // CHUNK
    io = (sum(o.dtype.itemsize for o in ops) + 4) * 2 * seq * PAIR
    scratch = 4 * seq * PAIR + 2 * nc * ((CHUNK + PAIR) * PAIR * 2 + PAIR * PAIR * 4) + 2 * PAIR * PAIR * 4
    pp = 1
    while pp * 2 <= n_pairs and n_pairs % (pp * 2) == 0 and (pp * 2) * (io + scratch) <= SCAN_VMEM_BUDGET:
        pp *= 2
    return pp


def _scan_call(ops, s0, want_final):
    bsz, n_pairs, seq, _ = ops[0].shape
    nc = seq // CHUNK
    pp = _pairs_per_step(n_pairs, seq, ops)
    assert (pp * nc) % min(ITEMS_PER_TRIP, pp * nc) == 0
    col_spec = pl.BlockSpec((None, pp, seq, PAIR), lambda b, g: (b, g, 0, 0))
    st_spec = pl.BlockSpec((None, 2, 2 * pp, HEAD_DIM, HEAD_DIM), lambda b, g: (b, 0, g, 0, 0))
    in_specs = [col_spec] * 9
    args = list(ops)
    has_s0 = s0 is not None
    if has_s0:
        in_specs.append(st_spec)
        args.append(s0)
    out_shape = [jax.ShapeDtypeStruct((bsz, n_pairs, seq, PAIR), F32)]
    out_specs = [col_spec]
    if want_final:
        out_shape.append(jax.ShapeDtypeStruct((bsz, 2, N_HEADS, HEAD_DIM, HEAD_DIM), F32))
        out_specs.append(st_spec)
    res = pl.pallas_call(
        functools.partial(_scan_kernel, has_s0=has_s0, want_final=want_final),
        out_shape=tuple(out_shape),
        grid=(bsz, n_pairs // pp),
        in_specs=in_specs,
        out_specs=tuple(out_specs),
        scratch_shapes=[pltpu.VMEM((pp, seq, PAIR), F32), pltpu.VMEM((pp, 2, PAIR, PAIR), F32),
                        pltpu.VMEM((pp, 2, nc, CHUNK + PAIR, PAIR), BF16), pltpu.VMEM((pp, 2, nc, PAIR, PAIR), F32)],
        compiler_params=pltpu.CompilerParams(dimension_semantics=("parallel", "parallel"),
                                             vmem_limit_bytes=VMEM_LIMIT),
        name="wkv_scan",
    )(*args)
    return res if want_final else (res[0], None)


def _post_kernel(x_ref, xp_ref, xn_ref, mod_ref, y_ref, bonus_ref, g1_ref, g2_ref, gf_ref,
                 w_pg_ref, mu_wag_ref, gw1_ref, gw2_ref, lnw_ref, lnb_ref, pool_w_ref, pool_scale_ref,
                 band_ref, icnt_ref, seg_ref, w_out_ref, w_ff1_ref, w_ff2_ref, o_ref, *, final_norm):
    d = x_ref.shape[-1]
    d_pool = len(POOL_WINDOWS) * POOL_GROUP
    mod = mod_ref[...]
    sh1, sc1, ga1, sh2, sc2, ga2 = (mod[:, i * d:(i + 1) * d] for i in range(N_MOD))
    x = x_ref[...]
    h, h_prev, h_next = _tile_h(x_ref, xp_ref, xn_ref, g1_ref[...], sc1, sh1)
    h_up, h_dn = _shift_rows(h, h_prev, h_next)
    xg = h + (0.5 * (h_up + h_dn) - h) * mu_wag_ref[2:3, :]

    z = _dot(h.astype(BF16), w_pg_ref[...])
    outs = []
    for g in range(len(POOL_WINDOWS)):
        zg = z[:, g * POOL_GROUP:(g + 1) * POOL_GROUP]
        zh = zg.astype(BF16)
        zl = (zg - zh.astype(F32)).astype(BF16)
        band = band_ref[g]
        mixed = (_dot(band, zh) + _dot(band, zl)) * icnt_ref[g] - zg
        outs.append(_dot(mixed.astype(BF16), pool_w_ref[g]))
    a_out = jnp.concatenate(outs, axis=-1) * pool_scale_ref[...]
    gate_a = _sigmoid(z[:, d_pool:d_pool + d])
    gate_b = _sigmoid(z[:, d_pool + d:d_pool + 2 * d])
    gl = _sigmoid(_dot(xg.astype(BF16), gw1_ref[...]))
    g_out = _dot(gl.astype(BF16), gw2_ref[...])

    seg = seg_ref[...]
    y = jnp.concatenate([y_ref[q] for q in range(y_ref.shape[0])], axis=-1)
    yc = y - _seg_sum(y, seg) * (1.0 / HEAD_DIM)
    var = _seg_sum(yc * yc, seg) * (1.0 / HEAD_DIM)
    yn = yc * lax.rsqrt(var + LNX_EPS) * lnw_ref[...] + lnb_ref[...]
    b_out = (yn + bonus_ref[...]) * g_out
    mix = _dot((gate_a * a_out + gate_b * b_out).astype(BF16), w_out_ref[...])
    x = x + ga1 * mix

    h2 = _modulated_norm(x, g2_ref[...], sc2, sh2)
    u = jnp.maximum(_dot(h2.astype(BF16), w_ff1_ref[...]), 0.0)
    ff = _dot((u * u).astype(BF16), w_ff2_ref[...])
    x = x + ga2 * ff
    if final_norm:
        x = x * lax.rsqrt(jnp.mean(x * x, axis=-1, keepdims=True) + RMS_EPS) * gf_ref[...]
    o_ref[...] = x


def _post_call(x, mod, y, bonus, p, consts, g_final, final_norm):
    bsz, seq, d = x.shape
    tm = TOKEN_TILE
    x_spec, xp_spec, xn_spec = _token_specs(tm, seq, d)
    mod_spec = pl.BlockSpec((None, 1, mod.shape[-1]), lambda b, j: (b % mod.shape[0], 0, 0))
    weights = (p["g1"], p["g2"], g_final, p["w_pg"], p["mu_wag"], p["gw1"], p["gw2"], p["ln_x_w"],
               p["ln_x_b"], p["pool_w"], p["pool_scale"], consts["band"], consts["icnt"], consts["seg"],
               p["w_out"], p["w_ff1"], p["w_ff2"])
    return pl.pallas_call(
        functools.partial(_post_kernel, final_norm=final_norm),
        out_shape=jax.ShapeDtypeStruct((bsz, seq, d), F32),
        grid=(bsz, seq // tm),
        in_specs=[x_spec, xp_spec, xn_spec, mod_spec, _pair_spec(tm, d // PAIR), x_spec]
        + [_const_spec(w.shape) for w in weights],
        out_specs=x_spec,
        compiler_params=pltpu.CompilerParams(dimension_semantics=("parallel", "parallel"),
                                             vmem_limit_bytes=VMEM_LIMIT),
        name="post_scan",
    )(x, x, x, mod, y, bonus, *weights)


def _pool_consts(tm, row_len):
    t = np.arange(tm)
    pos, row = t % row_len, t // row_len
    bands, icnts = [], []
    for win in POOL_WINDOWS:
        lo = np.clip(pos - win // 2, 0, row_len)
        hi = np.clip(pos + win - win // 2, 0, row_len)
        inside = (pos[None, :] >= lo[:, None]) & (pos[None, :] < hi[:, None]) & (row[None, :] == row[:, None])
        bands.append(inside.astype(np.float32))
        icnts.append(np.broadcast_to((1.0 / (hi - lo))[:, None], (tm, POOL_GROUP)).astype(np.float32))
    return jnp.asarray(np.stack(bands), BF16), jnp.asarray(np.stack(icnts), F32)


def _block_diag2(w):
    z = jnp.zeros_like(w[0])
    return jnp.concatenate([jnp.concatenate([w[0], z], axis=1), jnp.concatenate([z, w[1]], axis=1)], axis=0)


def _layer_params(li, w):
    d = w["w_in"].shape[1]
    d_pool = len(POOL_WINDOWS) * POOL_GROUP
    w_in = w["w_in"][li].astype(BF16)
    row = lambda a: a.reshape(1, -1)
    return {
        "g1": row(w["g_norm1"][li]), "g2": row(w["g_norm2"][li]),
        "w_rkv": w_in[:, d_pool:d_pool + 3 * d],
        "w_pg": jnp.concatenate([w_in[:, :d_pool], w_in[:, d_pool + 3 * d:]], axis=1),
        "mu_rkv": row(w["mu_rkv"][li]), "mu_wag": w["mu_wag"][li],
        "wd1": jnp.concatenate([w["w_dec1"][li, 0], w["w_dec1"][li, 1]], axis=1).astype(BF16),
        "wd2": _block_diag2(w["w_dec2"][li]).astype(BF16), "wd0": row(w["w_dec0"][li]),
        "a1": jnp.concatenate([w["a1"][li, 0], w["a1"][li, 1]], axis=1).astype(BF16),
        "a2": _block_diag2(w["a2"][li]).astype(BF16), "a0": row(w["a0"][li]),
        "gw1": w["gate_w1"][li].astype(BF16), "gw2": w["gate_w2"][li].astype(BF16),
        "k_k": row(w["k_k"][li]), "k_a": row(w["k_a"][li]), "r_k": row(w["r_k"][li]),
        "ln_x_w": row(w["ln_x_w"][li]), "ln_x_b": row(w["ln_x_b"][li]),
        "pool_w": w["pool_w"][li].astype(BF16), "pool_scale": row(w["pool_scale"][li]),
        "w_out": w["w_out"][li].astype(BF16), "w_ff1": w["w_ff1"][li].astype(BF16),
        "w_ff2": w["w_ff2"][li].astype(BF16),
    }


def _layer(x, mod, s0, want_final, p, consts, g_final, final_norm):
    ops = _pre_call(x, mod, p, consts)
    y, s_fin = _scan_call(ops[:9], s0, want_final)
    return _post_call(x, mod, y, ops[9], p, consts, g_final, final_norm), s_fin


def kernel(x_prompt, x_sample, state_rwkv, c, c_ctx, w_ada, b_ada, g_norm1, g_norm2, w_in, mu_rkv, mu_wag, w_dec0, w_dec1, w_dec2, a0, a1, a2, gate_w1, gate_w2, k_k, k_a, r_k, ln_x_w, ln_x_b, pool_w, pool_scale, w_out, w_ff1, w_ff2, g_final):
    w = dict(w_ada=w_ada, b_ada=b_ada, g_norm1=g_norm1, g_norm2=g_norm2, w_in=w_in, mu_rkv=mu_rkv,
             mu_wag=mu_wag, w_dec0=w_dec0, w_dec1=w_dec1, w_dec2=w_dec2, a0=a0, a1=a1, a2=a2,
             gate_w1=gate_w1, gate_w2=gate_w2, k_k=k_k, k_a=k_a, r_k=r_k, ln_x_w=ln_x_w, ln_x_b=ln_x_b,
             pool_w=pool_w, pool_scale=pool_scale, w_out=w_out, w_ff1=w_ff1, w_ff2=w_ff2)
    depth = w_in.shape[0]
    n_lat, lat_len, d = x_sample.shape
    ctx_len = x_prompt.shape[1]
    assert ctx_len % TOKEN_TILE == 0 and lat_len % TOKEN_TILE == 0 and TOKEN_TILE % GRID_W == 0
    assert ctx_len == TOKEN_TILE, "context pooling runs over the whole sequence inside one token tile"

    seg = jnp.asarray(np.kron(np.eye(SEG_TILE // HEAD_DIM), np.ones((HEAD_DIM, HEAD_DIM))), BF16)
    band_c, icnt_c = _pool_consts(TOKEN_TILE, ctx_len)
    band_l, icnt_l = _pool_consts(TOKEN_TILE, GRID_W)
    consts_ctx = {"seg": seg, "band": band_c, "icnt": icnt_c}
    consts_lat = {"seg": seg, "band": band_l, "icnt": icnt_l}

    n_cv = 1 + n_lat
    cv = jnp.concatenate([c_ctx[None, :], c, jnp.zeros((-n_cv % SUBLANES, d), F32)], axis=0)
    g_fin = g_final.reshape(1, d)
    ctx, lat = x_prompt, x_sample
    new_states = []
    for li in range(depth):
        p = _layer_params(li, w)
        mod = _ada_call(cv, w_ada[li], b_ada[li].reshape(1, -1))
        mod_ctx = mod[0:1].reshape(1, 1, -1)
        mod_lat = mod[1:n_cv].reshape(n_lat, 1, -1)
        last = li == depth - 1
        ctx, s_fin = _layer(ctx, mod_ctx, None, True, p, consts_ctx, g_fin, last)
        new_states.append(s_fin)
        lat, _ = _layer(lat, mod_lat, state_rwkv[:, li], False, p, consts_lat, g_fin, last)
    return ctx, lat, jnp.stack(new_states, axis=1)
```

```python
import functools

import numpy as np
import jax
import jax.numpy as jnp
from jax import lax
from jax.experimental import pallas as pl
from jax.experimental.pallas import tpu as pltpu

F32 = jnp.float32
BF16 = jnp.bfloat16

N_HEADS = 16
HEAD_DIM = 64
PAIR = 2 * HEAD_DIM
POOL_WINDOWS = (2, 4, 8, 16)
POOL_GROUP = 128
GRID_W = 64
N_MOD = 6
RMS_EPS = 1e-6
LNX_EPS = 64e-5
NORM_EPS = 1e-12

CHUNK = 64
ITEMS_PER_TRIP = 8
SCAN_VMEM_BUDGET = 46 * 1024 * 1024
TOKEN_TILE = 256
SEG_TILE = 256
SUBLANES = 8
VMEM_LIMIT = 56 * 1024 * 1024


def _dot(a, b):
    return jnp.dot(a, b, preferred_element_type=F32)


def _dot_nt(a, b):
    return lax.dot_general(a, b, (((1,), (1,)), ((), ())), preferred_element_type=F32)


def _dot_tn(a, b):
    return lax.dot_general(a, b, (((0,), (0,)), ((), ())), preferred_element_type=F32)


def _dot_f32(a, b):
    return jnp.dot(a, b, precision=lax.Precision.HIGHEST, preferred_element_type=F32)


def _split2(x):
    hi = x.astype(BF16)
    return hi, (x - hi.astype(F32)).astype(BF16)


def _dot_exact_rhs(x, m):
    hi, lo = _split2(x)
    return _dot(hi, m) + _dot(lo, m)


def _dot_exact_lhs(m, x):
    hi, lo = _split2(x)
    return _dot(m, hi) + _dot(m, lo)


def _seg_sum(x, seg_ones):
    d = x.shape[-1]
    parts = [_dot_exact_rhs(x[:, q:q + SEG_TILE], seg_ones) for q in range(0, d, SEG_TILE)]
    return jnp.concatenate(parts, axis=-1)


def _sigmoid(x):
    return 1.0 / (1.0 + jnp.exp(-x))


def _modulated_norm(x, g, scale, shift):
    y = x * lax.rsqrt(jnp.mean(x * x, axis=-1, keepdims=True) + RMS_EPS)
    return (y * g) * (1.0 + scale) + shift


def _shift_rows(v, first_row, last_row):
    n = v.shape[0]
    row = lax.broadcasted_iota(jnp.int32, v.shape, 0)
    up = jnp.where(row == 0, first_row, pltpu.roll(v, 1, 0))
    dn = jnp.where(row == n - 1, last_row, pltpu.roll(v, n - 1, 0))
    return up, dn


def _tile_h(x_ref, xp_ref, xn_ref, g, scale, shift):
    j = pl.program_id(1)
    nj = pl.num_programs(1)
    h = _modulated_norm(x_ref[...], g, scale, shift)
    halo = jnp.concatenate([xp_ref[SUBLANES - 1:SUBLANES, :], xn_ref[0:1, :]], axis=0)
    hh = _modulated_norm(halo, g, scale, shift)
    h_prev = jnp.where(j > 0, hh[0:1, :], 0.0)
    h_next = jnp.where(j < nj - 1, hh[1:2, :], 0.0)
    return h, h_prev, h_next


def _ada_kernel(cv_ref, w_ref, b_ref, o_ref):
    cv = cv_ref[...]
    o_ref[...] = _dot_f32(cv * _sigmoid(cv), w_ref[...]) + b_ref[...]


def _ada_call(cv, w_ada, b_ada):
    rows, d = cv.shape
    n = w_ada.shape[1]
    bn = n // 4
    return pl.pallas_call(
        _ada_kernel,
        out_shape=jax.ShapeDtypeStruct((rows, n), F32),
        grid=(n // bn,),
        in_specs=[pl.BlockSpec((rows, d), lambda i: (0, 0)),
                  pl.BlockSpec((d, bn), lambda i: (0, i)),
                  pl.BlockSpec((1, bn), lambda i: (0, i))],
        out_specs=pl.BlockSpec((rows, bn), lambda i: (0, i)),
        compiler_params=pltpu.CompilerParams(dimension_semantics=("arbitrary",),
                                             vmem_limit_bytes=VMEM_LIMIT),
        name="ada_mod",
    )(cv, w_ada, b_ada)


def _pre_kernel(x_ref, xp_ref, xn_ref, mod_ref, g1_ref, w_rkv_ref, mu_rkv_ref, mu_wag_ref,
                wd1_ref, wd2_ref, wd0_ref, a1_ref, a2_ref, a0_ref, kk_w_ref, ka_ref, rk_ref,
                seg_ref,
                lw0_ref, lw1_ref, r_ref, kd0_ref, kd1_ref, v_ref, kk_ref, b0_ref, b1_ref,
                bonus_ref):
    d = x_ref.shape[-1]
    mod = mod_ref[...]
    shift, scale = mod[:, 0:d], mod[:, d:2 * d]
    h, h_prev, h_next = _tile_h(x_ref, xp_ref, xn_ref, g1_ref[...], scale, shift)
    h_up, h_dn = _shift_rows(h, h_prev, h_next)
    hd = 0.5 * (h_up + h_dn) - h

    w_rkv = w_rkv_ref[...]
    z = _dot(h.astype(BF16), w_rkv)
    halo = jnp.concatenate([h_prev, h_next, jnp.zeros((SUBLANES - 2, d), F32)], axis=0)
    zh = _dot(halo.astype(BF16), w_rkv)
    z_up, z_dn = _shift_rows(z, zh[0:1, :], zh[1:2, :])
    mu_rkv = mu_rkv_ref[...]
    z = z * (1.0 - mu_rkv) + (z_up + z_dn) * (0.5 * mu_rkv)
    r, k, v = z[:, 0:d], z[:, d:2 * d], z[:, 2 * d:3 * d]

    mu = mu_wag_ref[...]
    xw = h + hd * mu[0:1, :]
    xa = h + hd * mu[1:2, :]

    tw = jnp.tanh(_dot(xw.astype(BF16), wd1_ref[...]))
    pre_w = wd0_ref[...] + _dot(tw.astype(BF16), wd2_ref[...])
    lw = -_sigmoid(pre_w) * float(np.exp(-0.5))
    ta = _dot(xa.astype(BF16), a1_ref[...])
    a = _sigmoid(a0_ref[...] + _dot(ta.astype(BF16), a2_ref[...]))

    seg = seg_ref[...]
    kkr = k * kk_w_ref[...]
    kk = kkr * lax.rsqrt(_seg_sum(kkr * kkr, seg) + NORM_EPS)
    ka = ka_ref[...]

    _store_pairs(lw0_ref, lw[:, 0:d])
    _store_pairs(lw1_ref, lw[:, d:2 * d])
    _store_pairs(r_ref, r)
    _store_pairs(v_ref, v)
    _store_pairs(kk_ref, kk)
    a_f, a_b = a[:, 0:d], a[:, d:2 * d]
    _store_pairs(kd0_ref, k * (1.0 + (a_f - 1.0) * ka))
    _store_pairs(kd1_ref, k * (1.0 + (a_b - 1.0) * ka))
    _store_pairs(b0_ref, kk * a_f)
    _store_pairs(b1_ref, kk * a_b)
    bonus_ref[...] = _seg_sum(r * k * rk_ref[...], seg) * v


def _store_pairs(ref, val):
    for q in range(ref.shape[0]):
        ref[q] = val[:, q * PAIR:(q + 1) * PAIR].astype(ref.dtype)


def _pair_spec(tm, n_pairs):
    return pl.BlockSpec((None, n_pairs, tm, PAIR), lambda b, j: (b, 0, j, 0))


def _const_spec(shape):
    nd = len(shape)
    return pl.BlockSpec(shape, lambda b, j: (0,) * nd, pipeline_mode=pl.Buffered(1))


def _token_specs(tm, seq, d):
    tiles8 = tm // SUBLANES
    last8 = seq // SUBLANES - 1
    x_spec = pl.BlockSpec((None, tm, d), lambda b, j: (b, j, 0))
    xp_spec = pl.BlockSpec((None, SUBLANES, d), lambda b, j: (b, jnp.maximum(j * tiles8 - 1, 0), 0))
    xn_spec = pl.BlockSpec((None, SUBLANES, d), lambda b, j: (b, jnp.minimum((j + 1) * tiles8, last8), 0))
    return x_spec, xp_spec, xn_spec


def _pre_call(x, mod, p, consts):
    bsz, seq, d = x.shape
    tm = TOKEN_TILE
    x_spec, xp_spec, xn_spec = _token_specs(tm, seq, d)
    mod_spec = pl.BlockSpec((None, 1, mod.shape[-1]), lambda b, j: (b % mod.shape[0], 0, 0))
    weights = (p["g1"], p["w_rkv"], p["mu_rkv"], p["mu_wag"], p["wd1"], p["wd2"], p["wd0"],
               p["a1"], p["a2"], p["a0"], p["k_k"], p["k_a"], p["r_k"], consts["seg"])
    n_pairs = d // PAIR
    pm = lambda dt: jax.ShapeDtypeStruct((bsz, n_pairs, seq, PAIR), dt)
    return pl.pallas_call(
        _pre_kernel,
        out_shape=(pm(F32), pm(F32)) + (pm(BF16),) * 7 + (jax.ShapeDtypeStruct((bsz, seq, d), F32),),
        grid=(bsz, seq // tm),
        in_specs=[x_spec, xp_spec, xn_spec, mod_spec] + [_const_spec(w.shape) for w in weights],
        out_specs=(_pair_spec(tm, n_pairs),) * 9 + (x_spec,),
        compiler_params=pltpu.CompilerParams(dimension_semantics=("parallel", "parallel"),
                                             vmem_limit_bytes=VMEM_LIMIT),
        name="pre_scan",
    )(x, x, x, mod, *weights)


def _stack(x, lane_head):
    return jnp.concatenate([jnp.where(lane_head == 0, x, 0.0), jnp.where(lane_head == 1, x, 0.0)], axis=0)


def _chunk_prep(insts, cst):
    lane_head, bd = cst["lane_head"], cst["bd"]
    c = insts[0][0].shape[0]
    ids = range(len(insts))
    rev = [t[6] for t in insts]
    stk = lambda x: _stack(x, lane_head)
    cum = [_dot_exact_lhs(cst["tri"][t[6]], t[0]) for t in insts]
    tot = [cum[i][(0 if rev[i] else c - 1):(1 if rev[i] else c), :] for i in ids]
    yield
    op = []
    for i in ids:
        lw, r, kd, v, kk, b = (t.astype(F32) for t in insts[i][:6])
        e_in = jnp.exp(cum[i])
        e_ex = jnp.exp(cum[i] - lw)
        e_ng = jnp.exp(-cum[i])
        e_end = jnp.exp(tot[i] - cum[i])
        a = kk * e_ex
        op.append(dict(
            r=r * e_in,
            ar=jnp.concatenate([a, r * e_in], axis=0).astype(BF16),
            bk=jnp.concatenate([stk(b * e_ng), stk(kd * e_ng)], axis=0).astype(BF16),
            a_st=stk(a).astype(BF16), v_st=stk(v).astype(BF16), v=v.astype(BF16),
            bh=(-(b * e_end)).astype(BF16), bkh=jnp.concatenate([-(b * e_end), kd * e_end], axis=0).astype(BF16)))
    sc = [_dot_nt(op[i]["ar"], op[i]["bk"]) for i in ids]
    yield
    strict = [cst["strict"][rev[i]] for i in ids]
    incl = [cst["incl"][rev[i]] for i in ids]
    l_w = [jnp.where(strict[i], -sc[i][:c, :PAIR], 0.0) for i in ids]
    m_kk = [jnp.concatenate([jnp.where(strict[i], sc[i][:c, PAIR:], 0.0),
                             jnp.where(incl[i], sc[i][c:, PAIR:], 0.0)], axis=0).astype(BF16) for i in ids]
    m_rb = [jnp.where(incl[i], -sc[i][c:, :PAIR], 0.0).astype(BF16) for i in ids]
    mvk = [_dot(m_kk[i], op[i]["v_st"]) for i in ids]
    yield

    x_w = [cst["eye_w"] + l_w[i] for i in ids]
    l_sp = [_split2(l_w[i]) for i in ids]
    p_w = [_dot(l_sp[i][0], stk(l_w[i]).astype(BF16)) for i in ids]
    yield
    n_sq = int(np.log2(c)) - 1
    for q in range(n_sq):
        p_bd = [stk(p_w[i]).astype(BF16) for i in ids]
        if q < n_sq - 1:
            zz = [_dot(jnp.concatenate([x_w[i], p_w[i]], axis=0).astype(BF16), p_bd[i]) for i in ids]
            x_w = [x_w[i] + zz[i][:c] for i in ids]
            p_w = [zz[i][c:] for i in ids]
            yield
        else:
            x_w = [x_w[i] + _dot(x_w[i].astype(BF16), p_bd[i]) for i in ids]
            yield
    t_sp = [_split2(stk(x_w[i])) for i in ids]
    lt = [_dot(jnp.concatenate(l_sp[i], axis=0), t_sp[i][0]) for i in ids]
    lt2 = [_dot(l_sp[i][0], t_sp[i][1]) for i in ids]
    yield
    e_w = [(cst["eye_w"] - x_w[i]) + (lt[i][:c] + lt[i][c:] + lt2[i]) for i in ids]
    x_w = [x_w[i] + _dot(x_w[i].astype(BF16), stk(e_w[i]).astype(BF16)) for i in ids]
    yield

    w_p = [_dot(x_w[i].astype(BF16), jnp.concatenate([op[i]["a_st"], stk(mvk[i][:c]).astype(BF16)], axis=1))
           for i in ids]
    yield
    w_st = [jnp.concatenate([stk(w_p[i][:, :PAIR]), stk(w_p[i][:, PAIR:])], axis=1).astype(BF16) for i in ids]
    ry = [_dot(m_rb[i], w_st[i]) for i in ids]
    g_f = [_dot_tn(op[i]["bh"], w_p[i][:, :PAIR].astype(BF16)) for i in ids]
    uv = [jnp.concatenate([w_p[i][:, PAIR:].astype(BF16), op[i]["v"]], axis=0) for i in ids]
    h_f = [_dot_tn(op[i]["bkh"], uv[i]) for i in ids]
    yield
    out = []
    for i in ids:
        r_p = op[i]["r"] + ry[i][:, :PAIR]
        g_m = jnp.where(cst["eye"], jnp.exp(tot[i]), 0.0) + jnp.where(bd, g_f[i], 0.0)
        out.append((jnp.concatenate([r_p, g_m], axis=0).astype(BF16), ry[i][:, PAIR:] + mvk[i][c:],
                    jnp.where(bd, h_f[i], 0.0)))
    return out


def _interleave(*gens):
    results = [None] * len(gens)
    live = dict(enumerate(gens))
    while live:
        for i in list(live):
            try:
                next(live[i])
            except StopIteration as done:
                results[i] = done.value
                del live[i]
    return results


def _scan_consts(c):
    n = 2 * c
    row = lax.broadcasted_iota(jnp.int32, (n, n), 0)
    col = lax.broadcasted_iota(jnp.int32, (n, n), 1)
    tr = lax.broadcasted_iota(jnp.int32, (c, c), 0)
    tc = lax.broadcasted_iota(jnp.int32, (c, c), 1)
    t = lax.broadcasted_iota(jnp.int32, (c, PAIR), 0)
    lane = lax.broadcasted_iota(jnp.int32, (c, PAIR), 1)
    s = lane % HEAD_DIM
    return {
        "tri": (jnp.where(tc <= tr, 1.0, 0.0).astype(BF16), jnp.where(tc >= tr, 1.0, 0.0).astype(BF16)),
        "strict": (s < t, s > t),
        "incl": (s <= t, s >= t),
        "eye_w": jnp.where(s == t, 1.0, 0.0),
        "eye": row == col,
        "bd": (row // c) == (col // c),
        "lane_head": lane // HEAD_DIM,
    }


def _scan_kernel(*refs, has_s0, want_final):
    lw0_ref, lw1_ref, r_ref, kd0_ref, kd1_ref, v_ref, kk_ref, b0_ref, b1_ref = refs[:9]
    pos = 9
    s0_ref = None
    if has_s0:
        s0_ref = refs[pos]
        pos += 1
    y_ref = refs[pos]
    pos += 1
    sf_ref = None
    if want_final:
        sf_ref = refs[pos]
        pos += 1
    yb_ref, st_ref, rg_ref, h_ref = refs[pos:pos + 4]

    c = CHUNK
    pp, seq = r_ref.shape[0], r_ref.shape[1]
    nc = seq // c
    ipt = min(ITEMS_PER_TRIP, pp * nc)
    cst = _scan_consts(c)
    zero = jnp.zeros((HEAD_DIM, HEAD_DIM), F32)
    lw_refs, kd_refs, b_refs, yv_refs = (lw0_ref, lw1_ref), (kd0_ref, kd1_ref), (b0_ref, b1_ref), (y_ref, yb_ref)

    for q in range(pp):
        for dr in range(2):
            if has_s0:
                s_a = s0_ref[dr, 2 * q].T
                s_b = s0_ref[dr, 2 * q + 1].T
                st_ref[q, dr] = jnp.concatenate([jnp.concatenate([s_a, zero], axis=1),
                                                 jnp.concatenate([zero, s_b], axis=1)], axis=0)
            else:
                st_ref[q, dr] = jnp.zeros((PAIR, PAIR), F32)

    cps = min(nc, ipt)
    ppt = ipt // cps
    tpp = nc // cps
    n_trips = (pp * nc) // ipt

    def trip_chunks(t):
        g, k = t // tpp, t % tpp
        return [[(g * ppt + u, dr, (k * cps + j) if dr == 0 else nc - 1 - (k * cps + j))
                 for u in range(ppt) for dr in range(2)] for j in range(cps)]

    def prep(t):
        dest = [x for step in trip_chunks(t) for x in step]
        insts = []
        for q, dr, ck in dest:
            rows = pl.ds(pl.multiple_of(ck * c, c), c)
            insts.append((lw_refs[dr][q, rows, :], r_ref[q, rows, :], kd_refs[dr][q, rows, :], v_ref[q, rows, :],
                          kk_ref[q, rows, :], b_refs[dr][q, rows, :], dr))
        outs = yield from _chunk_prep(insts, cst)
        for (q, dr, ck), (rg, y_v, h_m) in zip(dest, outs):
            rg_ref[q, dr, ck] = rg
            h_ref[q, dr, ck] = h_m
            yv_refs[dr][q, pl.ds(pl.multiple_of(ck * c, c), c), :] = y_v

    def chain(t):
        for step in trip_chunks(t):
            z = [_dot(rg_ref[q, dr, ck], st_ref[q, dr].astype(BF16)) for q, dr, ck in step]
            for (q, dr, ck), z_i in zip(step, z):
                rows = pl.ds(pl.multiple_of(ck * c, c), c)
                yv_refs[dr][q, rows, :] = yv_refs[dr][q, rows, :] + z_i[:c]
                st_ref[q, dr] = z_i[c:] + h_ref[q, dr, ck]
            yield

    _interleave(prep(0))

    def trip_body(t, carry):
        _interleave(chain(t - 1), prep(t))
        return carry

    lax.fori_loop(1, n_trips, trip_body, 0)
    _interleave(chain(n_trips - 1))
    y_ref[...] = y_ref[...] + yb_ref[...]
    if want_final:
        for q in range(pp):
            for dr in range(2):
                s_bd = st_ref[q, dr]
                sf_ref[dr, 2 * q] = s_bd[:HEAD_DIM, :HEAD_DIM].T
                sf_ref[dr, 2 * q + 1] = s_bd[HEAD_DIM:, HEAD_DIM:].T


def _pairs_per_step(n_pairs, seq, ops):
    nc = seq // CHUNK
    io = (sum(o.dtype.itemsize for o in ops) + 4) * 2 * seq * PAIR
    scratch = 4 * seq * PAIR + 2 * nc * ((CHUNK + PAIR) * PAIR * 2 + PAIR * PAIR * 4) + 2 * PAIR * PAIR * 4
    pp = 1
    while pp * 2 <= n_pairs and n_pairs % (pp * 2) == 0 and (pp * 2) * (io + scratch) <= SCAN_VMEM_BUDGET:
        pp *= 2
    return pp


def _scan_call(ops, s0, want_final):
    bsz, n_pairs, seq, _ = ops[0].shape
    nc = seq // CHUNK
    pp = _pairs_per_step(n_pairs, seq, ops)
    ipt = min(ITEMS_PER_TRIP, pp * nc)
    assert (pp * nc) % ipt == 0 and ipt % min(nc, ipt) == 0 and nc % min(nc, ipt) == 0
    col_spec = pl.BlockSpec((None, pp, seq, PAIR), lambda b, g: (b, g, 0, 0))
    st_spec = pl.BlockSpec((None, 2, 2 * pp, HEAD_DIM, HEAD_DIM), lambda b, g: (b, 0, g, 0, 0))
    in_specs = [col_spec] * 9
    args = list(ops)
    has_s0 = s0 is not None
    if has_s0:
        in_specs.append(st_spec)
        args.append(s0)
    out_shape = [jax.ShapeDtypeStruct((bsz, n_pairs, seq, PAIR), F32)]
    out_specs = [col_spec]
    if want_final:
        out_shape.append(jax.ShapeDtypeStruct((bsz, 2, N_HEADS, HEAD_DIM, HEAD_DIM), F32))
        out_specs.append(st_spec)
    res = pl.pallas_call(
        functools.partial(_scan_kernel, has_s0=has_s0, want_final=want_final),
        out_shape=tuple(out_shape),
        grid=(bsz, n_pairs // pp),
        in_specs=in_specs,
        out_specs=tuple(out_specs),
        scratch_shapes=[pltpu.VMEM((pp, seq, PAIR), F32), pltpu.VMEM((pp, 2, PAIR, PAIR), F32),
                        pltpu.VMEM((pp, 2, nc, CHUNK + PAIR, PAIR), BF16), pltpu.VMEM((pp, 2, nc, PAIR, PAIR), F32)],
        compiler_params=pltpu.CompilerParams(dimension_semantics=("parallel", "parallel"),
                                             vmem_limit_bytes=VMEM_LIMIT),
        name="wkv_scan",
    )(*args)
    return res if want_final else (res[0], None)


def _post_kernel(x_ref, xp_ref, xn_ref, mod_ref, y_ref, bonus_ref, g1_ref, g2_ref, gf_ref,
                 w_pg_ref, mu_wag_ref, gw1_ref, gw2_ref, lnw_ref, lnb_ref, pool_w_ref, pool_scale_ref,
                 band_ref, icnt_ref, seg_ref, w_out_ref, w_ff1_ref, w_ff2_ref, o_ref, *, final_norm):
    d = x_ref.shape[-1]
    d_pool = len(POOL_WINDOWS) * POOL_GROUP
    mod = mod_ref[...]
    sh1, sc1, ga1, sh2, sc2, ga2 = (mod[:, i * d:(i + 1) * d] for i in range(N_MOD))
    x = x_ref[...]
    h, h_prev, h_next = _tile_h(x_ref, xp_ref, xn_ref, g1_ref[...], sc1, sh1)
    h_up, h_dn = _shift_rows(h, h_prev, h_next)
    xg = h + (0.5 * (h_up + h_dn) - h) * mu_wag_ref[2:3, :]

    z = _dot(h.astype(BF16), w_pg_ref[...])
    outs = []
    for g in range(len(POOL_WINDOWS)):
        zg = z[:, g * POOL_GROUP:(g + 1) * POOL_GROUP]
        zh = zg.astype(BF16)
        zl = (zg - zh.astype(F32)).astype(BF16)
        band = band_ref[g]
        mixed = (_dot(band, zh) + _dot(band, zl)) * icnt_ref[g] - zg
        outs.append(_dot(mixed.astype(BF16), pool_w_ref[g]))
    a_out = jnp.concatenate(outs, axis=-1) * pool_scale_ref[...]
    gate_a = _sigmoid(z[:, d_pool:d_pool + d])
    gate_b = _sigmoid(z[:, d_pool + d:d_pool + 2 * d])
    gl = _sigmoid(_dot(xg.astype(BF16), gw1_ref[...]))
    g_out = _dot(gl.astype(BF16), gw2_ref[...])

    seg = seg_ref[...]
    y = jnp.concatenate([y_ref[q] for q in range(y_ref.shape[0])], axis=-1)
    yc = y - _seg_sum(y, seg) * (1.0 / HEAD_DIM)
    var = _seg_sum(yc * yc, seg) * (1.0 / HEAD_DIM)
    yn = yc * lax.rsqrt(var + LNX_EPS) * lnw_ref[...] + lnb_ref[...]
    b_out = (yn + bonus_ref[...]) * g_out
    mix = _dot((gate_a * a_out + gate_b * b_out).astype(BF16), w_out_ref[...])
    x = x + ga1 * mix

    h2 = _modulated_norm(x, g2_ref[...], sc2, sh2)
    u = jnp.maximum(_dot(h2.astype(BF16), w_ff1_ref[...]), 0.0)
    ff = _dot((u * u).astype(BF16), w_ff2_ref[...])
    x = x + ga2 * ff
    if final_norm:
        x = x * lax.rsqrt(jnp.mean(x * x, axis=-1, keepdims=True) + RMS_EPS) * gf_ref[...]
    o_ref[...] = x


def _post_call(x, mod, y, bonus, p, consts, g_final, final_norm):
    bsz, seq, d = x.shape
    tm = TOKEN_TILE
    x_spec, xp_spec, xn_spec = _token_specs(tm, seq, d)
    mod_spec = pl.BlockSpec((None, 1, mod.shape[-1]), lambda b, j: (b % mod.shape[0], 0, 0))
    weights = (p["g1"], p["g2"], g_final, p["w_pg"], p["mu_wag"], p["gw1"], p["gw2"], p["ln_x_w"],
               p["ln_x_b"], p["pool_w"], p["pool_scale"], consts["band"], consts["icnt"], consts["seg"],
               p["w_out"], p["w_ff1"], p["w_ff2"])
    return pl.pallas_call(
        functools.partial(_post_kernel, final_norm=final_norm),
        out_shape=jax.ShapeDtypeStruct((bsz, seq, d), F32),
        grid=(bsz, seq // tm),
        in_specs=[x_spec, xp_spec, xn_spec, mod_spec, _pair_spec(tm, d // PAIR), x_spec]
        + [_const_spec(w.shape) for w in weights],
        out_specs=x_spec,
        compiler_params=pltpu.CompilerParams(dimension_semantics=("parallel", "parallel"),
                                             vmem_limit_bytes=VMEM_LIMIT),
        name="post_scan",
    )(x, x, x, mod, y, bonus, *weights)


def _pool_consts(tm, row_len):
    t = np.arange(tm)
    pos, row = t % row_len, t // row_len
    bands, icnts = [], []
    for win in POOL_WINDOWS:
        lo = np.clip(pos - win // 2, 0, row_len)
        hi = np.clip(pos + win - win // 2, 0, row_len)
        inside = (pos[None, :] >= lo[:, None]) & (pos[None, :] < hi[:, None]) & (row[None, :] == row[:, None])
        bands.append(inside.astype(np.float32))
        icnts.append(np.broadcast_to((1.0 / (hi - lo))[:, None], (tm, POOL_GROUP)).astype(np.float32))
    return jnp.asarray(np.stack(bands), BF16), jnp.asarray(np.stack(icnts), F32)


def _block_diag2(w):
    z = jnp.zeros_like(w[0])
    return jnp.concatenate([jnp.concatenate([w[0], z], axis=1), jnp.concatenate([z, w[1]], axis=1)], axis=0)


def _layer_params(li, w):
    d = w["w_in"].shape[1]
    d_pool = len(POOL_WINDOWS) * POOL_GROUP
    w_in = w["w_in"][li].astype(BF16)
    row = lambda a: a.reshape(1, -1)
    return {
        "g1": row(w["g_norm1"][li]), "g2": row(w["g_norm2"][li]),
        "w_rkv": w_in[:, d_pool:d_pool + 3 * d],
        "w_pg": jnp.concatenate([w_in[:, :d_pool], w_in[:, d_pool + 3 * d:]], axis=1),
        "mu_rkv": row(w["mu_rkv"][li]), "mu_wag": w["mu_wag"][li],
        "wd1": jnp.concatenate([w["w_dec1"][li, 0], w["w_dec1"][li, 1]], axis=1).astype(BF16),
        "wd2": _block_diag2(w["w_dec2"][li]).astype(BF16), "wd0": row(w["w_dec0"][li]),
        "a1": jnp.concatenate([w["a1"][li, 0], w["a1"][li, 1]], axis=1).astype(BF16),
        "a2": _block_diag2(w["a2"][li]).astype(BF16), "a0": row(w["a0"][li]),
        "gw1": w["gate_w1"][li].astype(BF16), "gw2": w["gate_w2"][li].astype(BF16),
        "k_k": row(w["k_k"][li]), "k_a": row(w["k_a"][li]), "r_k": row(w["r_k"][li]),
        "ln_x_w": row(w["ln_x_w"][li]), "ln_x_b": row(w["ln_x_b"][li]),
        "pool_w": w["pool_w"][li].astype(BF16), "pool_scale": row(w["pool_scale"][li]),
        "w_out": w["w_out"][li].astype(BF16), "w_ff1": w["w_ff1"][li].astype(BF16),
        "w_ff2": w["w_ff2"][li].astype(BF16),
    }


def _layer(x, mod, s0, want_final, p, consts, g_final, final_norm):
    ops = _pre_call(x, mod, p, consts)
    y, s_fin = _scan_call(ops[:9], s0, want_final)
    return _post_call(x, mod, y, ops[9], p, consts, g_final, final_norm), s_fin


def kernel(x_prompt, x_sample, state_rwkv, c, c_ctx, w_ada, b_ada, g_norm1, g_norm2, w_in, mu_rkv, mu_wag, w_dec0, w_dec1, w_dec2, a0, a1, a2, gate_w1, gate_w2, k_k, k_a, r_k, ln_x_w, ln_x_b, pool_w, pool_scale, w_out, w_ff1, w_ff2, g_final):
    w = dict(w_ada=w_ada, b_ada=b_ada, g_norm1=g_norm1, g_norm2=g_norm2, w_in=w_in, mu_rkv=mu_rkv,
             mu_wag=mu_wag, w_dec0=w_dec0, w_dec1=w_dec1, w_dec2=w_dec2, a0=a0, a1=a1, a2=a2,
             gate_w1=gate_w1, gate_w2=gate_w2, k_k=k_k, k_a=k_a, r_k=r_k, ln_x_w=ln_x_w, ln_x_b=ln_x_b,
             pool_w=pool_w, pool_scale=pool_scale, w_out=w_out, w_ff1=w_ff1, w_ff2=w_ff2)
    depth = w_in.shape[0]
    n_lat, lat_len, d = x_sample.shape
    ctx_len = x_prompt.shape[1]
    assert ctx_len % TOKEN_TILE == 0 and lat_len % TOKEN_TILE == 0 and TOKEN_TILE % GRID_W == 0
    assert ctx_len == TOKEN_TILE, "context pooling runs over the whole sequence inside one token tile"

    seg = jnp.asarray(np.kron(np.eye(SEG_TILE // HEAD_DIM), np.ones((HEAD_DIM, HEAD_DIM))), BF16)
    band_c, icnt_c = _pool_consts(TOKEN_TILE, ctx_len)
    band_l, icnt_l = _pool_consts(TOKEN_TILE, GRID_W)
    consts_ctx = {"seg": seg, "band": band_c, "icnt": icnt_c}
    consts_lat = {"seg": seg, "band": band_l, "icnt": icnt_l}

    n_cv = 1 + n_lat
    cv = jnp.concatenate([c_ctx[None, :], c, jnp.zeros((-n_cv % SUBLANES, d), F32)], axis=0)
    g_fin = g_final.reshape(1, d)
    ctx, lat = x_prompt, x_sample
    new_states = []
    for li in range(depth):
        p = _layer_params(li, w)
        mod = _ada_call(cv, w_ada[li], b_ada[li].reshape(1, -1))
        mod_ctx = mod[0:1].reshape(1, 1, -1)
        mod_lat = mod[1:n_cv].reshape(n_lat, 1, -1)
        last = li == depth - 1
        ctx, s_fin = _layer(ctx, mod_ctx, None, True, p, consts_ctx, g_fin, last)
        new_states.append(s_fin)
        lat, _ = _layer(lat, mod_lat, state_rwkv[:, li], False, p, consts_lat, g_fin, last)
    return ctx, lat, jnp.stack(new_states, axis=1)
```

```python
import functools

import numpy as np
import jax
import jax.numpy as jnp
from jax import lax
from jax.experimental import pallas as pl
from jax.experimental.pallas import tpu as pltpu

F32 = jnp.float32
BF16 = jnp.bfloat16

N_HEADS = 16
HEAD_DIM = 64
PAIR = 2 * HEAD_DIM
POOL_WINDOWS = (2, 4, 8, 16)
POOL_GROUP = 128
GRID_W = 64
N_MOD = 6
RMS_EPS = 1e-6
LNX_EPS = 64e-5
NORM_EPS = 1e-12

CHUNK = 64
SCAN_OPERAND_DTYPES = (F32, F32, BF16, F32, F32, BF16, BF16, F32, F32)
BASE_BLOCK = 8
ITEMS_PER_TRIP = 8
SCAN_VMEM_BUDGET = 50 * 1024 * 1024
TOKEN_TILE = 256
SEG_TILE = 256
SUBLANES = 8
VMEM_LIMIT = 58 * 1024 * 1024


def _dot(a, b):
    return jnp.dot(a, b, preferred_element_type=F32)


def _dot_nt(a, b):
    return lax.dot_general(a, b, (((1,), (1,)), ((), ())), preferred_element_type=F32)


def _dot_tn(a, b):
    return lax.dot_general(a, b, (((0,), (0,)), ((), ())), preferred_element_type=F32)


def _dot_f32(a, b):
    return jnp.dot(a, b, precision=lax.Precision.HIGHEST, preferred_element_type=F32)


def _split2(x):
    hi = x.astype(BF16)
    return hi, (x - hi.astype(F32)).astype(BF16)


def _dot_exact_rhs(x, m):
    hi, lo = _split2(x)
    return _dot(hi, m) + _dot(lo, m)


def _dot_exact_lhs(m, x):
    hi, lo = _split2(x)
    return _dot(m, hi) + _dot(m, lo)


def _seg_sum(x, seg_ones):
    d = x.shape[-1]
    parts = [_dot_exact_rhs(x[:, q:q + SEG_TILE], seg_ones) for q in range(0, d, SEG_TILE)]
    return jnp.concatenate(parts, axis=-1)


def _sigmoid(x):
    return 1.0 / (1.0 + jnp.exp(-x))


def _modulated_norm(x, g, scale, shift):
    y = x * lax.rsqrt(jnp.mean(x * x, axis=-1, keepdims=True) + RMS_EPS)
    return (y * g) * (1.0 + scale) + shift


def _shift_rows(v, first_row, last_row):
    n = v.shape[0]
    row = lax.broadcasted_iota(jnp.int32, v.shape, 0)
    up = jnp.where(row == 0, first_row, pltpu.roll(v, 1, 0))
    dn = jnp.where(row == n - 1, last_row, pltpu.roll(v, n - 1, 0))
    return up, dn


def _tile_h(x_ref, xp_ref, xn_ref, g, scale, shift):
    j = pl.program_id(1)
    nj = pl.num_programs(1)
    h = _modulated_norm(x_ref[...], g, scale, shift)
    halo = jnp.concatenate([xp_ref[SUBLANES - 1:SUBLANES, :], xn_ref[0:1, :]], axis=0)
    hh = _modulated_norm(halo, g, scale, shift)
    h_prev = jnp.where(j > 0, hh[0:1, :], 0.0)
    h_next = jnp.where(j < nj - 1, hh[1:2, :], 0.0)
    return h, h_prev, h_next


def _ada_kernel(cv_ref, w_ref, b_ref, o_ref):
    cv = cv_ref[...]
    o_ref[...] = _dot_f32(cv * _sigmoid(cv), w_ref[...]) + b_ref[...]


def _ada_call(cv, w_ada, b_ada):
    rows, d = cv.shape
    n = w_ada.shape[1]
    bn = n // 4
    return pl.pallas_call(
        _ada_kernel,
        out_shape=jax.ShapeDtypeStruct((rows, n), F32),
        grid=(n // bn,),
        in_specs=[pl.BlockSpec((rows, d), lambda i: (0, 0)),
                  pl.BlockSpec((d, bn), lambda i: (0, i)),
                  pl.BlockSpec((1, bn), lambda i: (0, i))],
        out_specs=pl.BlockSpec((rows, bn), lambda i: (0, i)),
        compiler_params=pltpu.CompilerParams(dimension_semantics=("arbitrary",),
                                             vmem_limit_bytes=VMEM_LIMIT),
        name="ada_mod",
    )(cv, w_ada, b_ada)


def _pre_kernel(x_ref, xp_ref, xn_ref, mod_ref, g1_ref, w_in_ref, mu_rkv_ref, mu_wag_ref,
                wd1_ref, wd2_ref, wd0_ref, a1_ref, a2_ref, a0_ref, kk_w_ref, ka_ref, rk_ref,
                seg_ref,
                lw0_ref, lw1_ref, r_ref, kd0_ref, kd1_ref, v_ref, kk_ref, b0_ref, b1_ref,
                bonus_ref):
    d = x_ref.shape[-1]
    mod = mod_ref[...]
    shift, scale = mod[:, 0:d], mod[:, d:2 * d]
    h, h_prev, h_next = _tile_h(x_ref, xp_ref, xn_ref, g1_ref[...], scale, shift)
    h_up, h_dn = _shift_rows(h, h_prev, h_next)
    hd = 0.5 * (h_up + h_dn) - h

    d_pool = len(POOL_WINDOWS) * POOL_GROUP
    w_rkv = w_in_ref[:, d_pool:d_pool + 3 * d]
    z = _dot(h.astype(BF16), w_rkv)
    halo = jnp.concatenate([h_prev, h_next, jnp.zeros((SUBLANES - 2, d), F32)], axis=0)
    zh = _dot(halo.astype(BF16), w_rkv)
    z_up, z_dn = _shift_rows(z, zh[0:1, :], zh[1:2, :])
    mu_rkv = mu_rkv_ref[...]
    z = z * (1.0 - mu_rkv) + (z_up + z_dn) * (0.5 * mu_rkv)
    r, k, v = z[:, 0:d], z[:, d:2 * d], z[:, 2 * d:3 * d]

    mu = mu_wag_ref[...]
    xw = h + hd * mu[0:1, :]
    xa = h + hd * mu[1:2, :]

    tw = jnp.tanh(_dot(xw.astype(BF16), wd1_ref[...]))
    pre_w = wd0_ref[...] + _dot(tw.astype(BF16), wd2_ref[...])
    lw = -_sigmoid(pre_w) * float(np.exp(-0.5))
    ta = _dot(xa.astype(BF16), a1_ref[...])
    a = _sigmoid(a0_ref[...] + _dot(ta.astype(BF16), a2_ref[...]))

    seg = seg_ref[...]
    kkr = k * kk_w_ref[...]
    kk = kkr * lax.rsqrt(_seg_sum(kkr * kkr, seg) + NORM_EPS)
    ka = ka_ref[...]

    _store_pairs(lw0_ref, lw[:, 0:d])
    _store_pairs(lw1_ref, lw[:, d:2 * d])
    _store_pairs(r_ref, r)
    _store_pairs(v_ref, v)
    _store_pairs(kk_ref, kk)
    a_f, a_b = a[:, 0:d], a[:, d:2 * d]
    _store_pairs(kd0_ref, k * (1.0 + (a_f - 1.0) * ka))
    _store_pairs(kd1_ref, k * (1.0 + (a_b - 1.0) * ka))
    _store_pairs(b0_ref, kk * a_f)
    _store_pairs(b1_ref, kk * a_b)
    bonus_ref[...] = _seg_sum(r * k * rk_ref[...], seg) * v


def _store_pairs(ref, val):
    for q in range(ref.shape[0]):
        ref[q] = val[:, q * PAIR:(q + 1) * PAIR].astype(ref.dtype)


def _pair_spec(tm, n_pairs):
    return pl.BlockSpec((None, n_pairs, tm, PAIR), lambda b, j: (b, 0, j, 0))


def _const_spec(shape):
    nd = len(shape)
    return pl.BlockSpec(shape, lambda b, j: (0,) * nd, pipeline_mode=pl.Buffered(1))


def _token_specs(tm, seq, d):
    tiles8 = tm // SUBLANES
    last8 = seq // SUBLANES - 1
    x_spec = pl.BlockSpec((None, tm, d), lambda b, j: (b, j, 0))
    xp_spec = pl.BlockSpec((None, SUBLANES, d), lambda b, j: (b, jnp.maximum(j * tiles8 - 1, 0), 0))
    xn_spec = pl.BlockSpec((None, SUBLANES, d), lambda b, j: (b, jnp.minimum((j + 1) * tiles8, last8), 0))
    return x_spec, xp_spec, xn_spec


def _pre_call(x, mod, p, consts):
    bsz, seq, d = x.shape
    tm = TOKEN_TILE
    x_spec, xp_spec, xn_spec = _token_specs(tm, seq, d)
    mod_spec = pl.BlockSpec((None, 1, mod.shape[-1]), lambda b, j: (b % mod.shape[0], 0, 0))
    weights = (p["g1"], p["w_in"], p["mu_rkv"], p["mu_wag"], p["wd1"], p["wd2"], p["wd0"],
               p["a1"], p["a2"], p["a0"], p["k_k"], p["k_a"], p["r_k"], consts["seg"])
    n_pairs = d // PAIR
    pm = lambda dt: jax.ShapeDtypeStruct((bsz, n_pairs, seq, PAIR), dt)
    return pl.pallas_call(
        _pre_kernel,
        out_shape=tuple(pm(dt) for dt in SCAN_OPERAND_DTYPES) + (jax.ShapeDtypeStruct((bsz, seq, d), F32),),
        grid=(bsz, seq // tm),
        in_specs=[x_spec, xp_spec, xn_spec, mod_spec] + [_const_spec(w.shape) for w in weights],
        out_specs=(_pair_spec(tm, n_pairs),) * 9 + (x_spec,),
        compiler_params=pltpu.CompilerParams(dimension_semantics=("parallel", "parallel"),
                                             vmem_limit_bytes=VMEM_LIMIT),
        name="pre_scan",
    )(x, x, x, mod, *weights)


def _stack(x, lane_head):
    zero = jnp.zeros_like(x)
    return jnp.concatenate([jnp.where(lane_head == 0, x, zero), jnp.where(lane_head == 1, x, zero)], axis=0)


def _chunk_prep(insts, cst):
    lane_head, bd = cst["lane_head"], cst["bd"]
    c = insts[0][0].shape[0]
    ids = range(len(insts))
    rev = [t[6] for t in insts]
    stk = lambda x: _stack(x, lane_head)
    cum = [_dot_exact_lhs(cst["tri"][t[6]], t[0]) for t in insts]
    tot = [cum[i][(0 if rev[i] else c - 1):(1 if rev[i] else c), :] for i in ids]
    yield
    op = []
    for i in ids:
        lw, r, kd, v, kk, b = (t.astype(F32) for t in insts[i][:6])
        e_in = jnp.exp(cum[i])
        e_ex = jnp.exp(cum[i] - lw)
        e_ng = jnp.exp(-cum[i])
        e_end = jnp.exp(tot[i] - cum[i])
        a = kk * e_ex
        op.append(dict(
            r=r * e_in,
            ar=jnp.concatenate([a, r * e_in], axis=0).astype(BF16),
            bk=jnp.concatenate([stk(b * e_ng), stk(kd * e_ng)], axis=0).astype(BF16),
            a_st=stk(a).astype(BF16), v_st=stk(v).astype(BF16), v=v.astype(BF16),
            bh=(-(b * e_end)).astype(BF16), bkh=jnp.concatenate([-(b * e_end), kd * e_end], axis=0).astype(BF16)))
    sc = [_dot_nt(op[i]["ar"], op[i]["bk"]) for i in ids]
    yield
    strict = [cst["strict"][rev[i]] for i in ids]
    incl = [cst["incl"][rev[i]] for i in ids]
    l_w = [jnp.where(strict[i], -sc[i][:c, :PAIR], 0.0) for i in ids]
    m_kk = [jnp.concatenate([jnp.where(strict[i], sc[i][:c, PAIR:], 0.0),
                             jnp.where(incl[i], sc[i][c:, PAIR:], 0.0)], axis=0).astype(BF16) for i in ids]
    m_rb = [jnp.where(incl[i], -sc[i][c:, :PAIR], 0.0).astype(BF16) for i in ids]
    mvk = [_dot(m_kk[i], op[i]["v_st"]) for i in ids]
    yield

    x_w = []
    for i in ids:
        d_w = jnp.where(cst["blk"][BASE_BLOCK], l_w[i], 0.0)
        x_w.append((cst["eye_w"] + d_w, d_w.astype(BF16)))
    p_w = [_dot(x_w[i][1], stk(x_w[i][1])) for i in ids]
    x_w = [x_w[i][0] for i in ids]
    yield
    n_sq = int(np.log2(BASE_BLOCK)) - 1
    for q in range(n_sq):
        p_bd = [stk(p_w[i].astype(BF16)) for i in ids]
        if q < n_sq - 1:
            zz = [_dot(jnp.concatenate([x_w[i], p_w[i]], axis=0).astype(BF16), p_bd[i]) for i in ids]
            x_w = [x_w[i] + zz[i][:c] for i in ids]
            p_w = [zz[i][c:] for i in ids]
        else:
            x_w = [x_w[i] + _dot(x_w[i].astype(BF16), p_bd[i]) for i in ids]
        yield
    m = BASE_BLOCK
    while m < c:
        f_bd = [stk(jnp.where(cst["blk"][2 * m] & ~cst["blk"][m], l_w[i], 0.0).astype(BF16)) for i in ids]
        t_b = [x_w[i].astype(BF16) for i in ids]
        u_w = [_dot(t_b[i], f_bd[i]) for i in ids]
        yield
        x_w = [x_w[i] + _dot(u_w[i].astype(BF16), stk(t_b[i])) for i in ids]
        yield
        m *= 2

    w_p = [_dot(x_w[i].astype(BF16), jnp.concatenate([op[i]["a_st"], stk(mvk[i][:c]).astype(BF16)], axis=1))
           for i in ids]
    yield
    w_st = [jnp.concatenate([stk(w_p[i][:, :PAIR]), stk(w_p[i][:, PAIR:])], axis=1).astype(BF16) for i in ids]
    ry = [_dot(m_rb[i], w_st[i]) for i in ids]
    g_f = [_dot_tn(op[i]["bh"], w_p[i][:, :PAIR].astype(BF16)) for i in ids]
    uv = [jnp.concatenate([w_p[i][:, PAIR:].astype(BF16), op[i]["v"]], axis=0) for i in ids]
    h_f = [_dot_tn(op[i]["bkh"], uv[i]) for i in ids]
    yield
    out = []
    for i in ids:
        r_p = op[i]["r"] + ry[i][:, :PAIR]
        g_m = jnp.where(cst["eye"], jnp.exp(tot[i]), 0.0) + jnp.where(bd, g_f[i], 0.0)
        h_bd = jnp.where(bd, h_f[i], 0.0)
        out.append((jnp.concatenate([r_p, g_m], axis=0).astype(BF16), ry[i][:, PAIR:] + mvk[i][c:],
                    h_bd[:c] + h_bd[c:]))
    return out


def _interleave(*gens):
    results = [None] * len(gens)
    live = dict(enumerate(gens))
    while live:
        for i in list(live):
            try:
                next(live[i])
            except StopIteration as done:
                results[i] = done.value
                del live[i]
    return results


def _scan_consts(c):
    n = 2 * c
    row = lax.broadcasted_iota(jnp.int32, (n, n), 0)
    col = lax.broadcasted_iota(jnp.int32, (n, n), 1)
    tr = lax.broadcasted_iota(jnp.int32, (c, c), 0)
    tc = lax.broadcasted_iota(jnp.int32, (c, c), 1)
    t = lax.broadcasted_iota(jnp.int32, (c, PAIR), 0)
    lane = lax.broadcasted_iota(jnp.int32, (c, PAIR), 1)
    s = lane % HEAD_DIM
    return {
        "tri": (jnp.where(tc <= tr, 1.0, 0.0).astype(BF16), jnp.where(tc >= tr, 1.0, 0.0).astype(BF16)),
        "strict": (s < t, s > t),
        "incl": (s <= t, s >= t),
        "eye_w": jnp.where(s == t, 1.0, 0.0),
        "blk": {m: (s // m) == (t // m) for m in (BASE_BLOCK << e for e in range(int(np.log2(c // BASE_BLOCK)) + 1))},
        "eye": row == col,
        "bd": (row // c) == (col // c),
        "lane_head": lane // HEAD_DIM,
    }


def _scan_kernel(*refs, has_s0, want_final):
    lw0_ref, lw1_ref, r_ref, kd0_ref, kd1_ref, v_ref, kk_ref, b0_ref, b1_ref = refs[:9]
    pos = 9
    s0_ref = None
    if has_s0:
        s0_ref = refs[pos]
        pos += 1
    y_ref = refs[pos]
    pos += 1
    sf_ref = None
    if want_final:
        sf_ref = refs[pos]
        pos += 1
    yb_ref, st_ref, rg_ref, h_ref = refs[pos:pos + 4]

    c = CHUNK
    pp, seq = r_ref.shape[0], r_ref.shape[1]
    nc = seq // c
    ipt = min(ITEMS_PER_TRIP, pp * nc)
    cst = _scan_consts(c)
    zero = jnp.zeros((HEAD_DIM, HEAD_DIM), F32)
    lw_refs, kd_refs, b_refs, yv_refs = (lw0_ref, lw1_ref), (kd0_ref, kd1_ref), (b0_ref, b1_ref), (y_ref, yb_ref)

    for q in range(pp):
        for dr in range(2):
            if has_s0:
                s_a = s0_ref[dr, 2 * q].T
                s_b = s0_ref[dr, 2 * q + 1].T
                st_ref[q, dr] = jnp.concatenate([jnp.concatenate([s_a, zero], axis=1),
                                                 jnp.concatenate([zero, s_b], axis=1)], axis=0)
            else:
                st_ref[q, dr] = jnp.zeros((PAIR, PAIR), F32)

    cps = min(nc, ipt)
    ppt = ipt // cps
    tpp = nc // cps
    n_trips = (pp * nc) // ipt

    def trip_chunks(t):
        g, k = t // tpp, t % tpp
        return [[(g * ppt + u, dr, (k * cps + j) if dr == 0 else nc - 1 - (k * cps + j))
                 for u in range(ppt) for dr in range(2)] for j in range(cps)]

    def prep(t):
        dest = [x for step in trip_chunks(t) for x in step]
        insts = []
        for q, dr, ck in dest:
            rows = pl.ds(pl.multiple_of(ck * c, c), c)
            insts.append((lw_refs[dr][q, rows, :], r_ref[q, rows, :], kd_refs[dr][q, rows, :], v_ref[q, rows, :],
                          kk_ref[q, rows, :], b_refs[dr][q, rows, :], dr))
        outs = yield from _chunk_prep(insts, cst)
        for (q, dr, ck), (rg, y_v, h_m) in zip(dest, outs):
            rg_ref[q, dr, ck] = rg
            h_ref[q, dr, ck] = h_m
            yv_refs[dr][q, pl.ds(pl.multiple_of(ck * c, c), c), :] = y_v

    def chain(t):
        for step in trip_chunks(t):
            z = [_dot(rg_ref[q, dr, ck], st_ref[q, dr].astype(BF16)) for q, dr, ck in step]
            for (q, dr, ck), z_i in zip(step, z):
                rows = pl.ds(pl.multiple_of(ck * c, c), c)
                yv_refs[dr][q, rows, :] = yv_refs[dr][q, rows, :] + z_i[:c]
                h_w = h_ref[q, dr, ck]
                st_ref[q, dr] = z_i[c:] + jnp.where(cst["bd"], jnp.concatenate([h_w, h_w], axis=0), 0.0)
            yield

    _interleave(prep(0))

    def trip_body(t, carry):
        _interleave(chain(t - 1), prep(t))
        return carry

    lax.fori_loop(1, n_trips, trip_body, 0)
    _interleave(chain(n_trips - 1))
    y_ref[...] = y_ref[...] + yb_ref[...]
    if want_final:
        for q in range(pp):
            for dr in range(2):
                s_bd = st_ref[q, dr]
                sf_ref[dr, 2 * q] = s_bd[:HEAD_DIM, :HEAD_DIM].T
                sf_ref[dr, 2 * q + 1] = s_bd[HEAD_DIM:, HEAD_DIM:].T


def _pairs_per_step(n_pairs, seq, ops):
    nc = seq // CHUNK
    io = (sum(o.dtype.itemsize for o in ops) + 4) * 2 * seq * PAIR
    scratch = 4 * seq * PAIR + 2 * nc * ((CHUNK + PAIR) * PAIR * 2 + CHUNK * PAIR * 4) + 2 * PAIR * PAIR * 4
    pp = 1
    while pp * 2 <= n_pairs and n_pairs % (pp * 2) == 0 and (pp * 2) * (io + scratch) <= SCAN_VMEM_BUDGET:
        pp *= 2
    return pp


def _scan_call(ops, s0, want_final):
    bsz, n_pairs, seq, _ = ops[0].shape
    nc = seq // CHUNK
    pp = _pairs_per_step(n_pairs, seq, ops)
    ipt = min(ITEMS_PER_TRIP, pp * nc)
    assert (pp * nc) % ipt == 0 and ipt % min(nc, ipt) == 0 and nc % min(nc, ipt) == 0
    col_spec = pl.BlockSpec((None, pp, seq, PAIR), lambda b, g: (b, g, 0, 0))
    st_spec = pl.BlockSpec((None, 2, 2 * pp, HEAD_DIM, HEAD_DIM), lambda b, g: (b, 0, g, 0, 0))
    in_specs = [col_spec] * 9
    args = list(ops)
    has_s0 = s0 is not None
    if has_s0:
        in_specs.append(st_spec)
        args.append(s0)
    out_shape = [jax.ShapeDtypeStruct((bsz, n_pairs, seq, PAIR), F32)]
    out_specs = [col_spec]
    if want_final:
        out_shape.append(jax.ShapeDtypeStruct((bsz, 2, N_HEADS, HEAD_DIM, HEAD_DIM), F32))
        out_specs.append(st_spec)
    res = pl.pallas_call(
        functools.partial(_scan_kernel, has_s0=has_s0, want_final=want_final),
        out_shape=tuple(out_shape),
        grid=(bsz, n_pairs // pp),
        in_specs=in_specs,
        out_specs=tuple(out_specs),
        scratch_shapes=[pltpu.VMEM((pp, seq, PAIR), F32), pltpu.VMEM((pp, 2, PAIR, PAIR), F32),
                        pltpu.VMEM((pp, 2, nc, CHUNK + PAIR, PAIR), BF16), pltpu.VMEM((pp, 2, nc, CHUNK, PAIR), F32)],
        compiler_params=pltpu.CompilerParams(dimension_semantics=("parallel", "parallel"),
                                             vmem_limit_bytes=VMEM_LIMIT),
        name="wkv_scan",
    )(*args)
    return res if want_final else (res[0], None)


def _post_kernel(x_ref, xp_ref, xn_ref, mod_ref, y_ref, bonus_ref, g1_ref, g2_ref, gf_ref,
                 w_in_ref, mu_wag_ref, gw1_ref, gw2_ref, lnw_ref, lnb_ref, pool_w_ref, pool_scale_ref,
                 band_ref, icnt_ref, seg_ref, w_out_ref, w_ff1_ref, w_ff2_ref, o_ref, *, final_norm):
    d = x_ref.shape[-1]
    d_pool = len(POOL_WINDOWS) * POOL_GROUP
    mod = mod_ref[...]
    sh1, sc1, ga1, sh2, sc2, ga2 = (mod[:, i * d:(i + 1) * d] for i in range(N_MOD))
    x = x_ref[...]
    h, h_prev, h_next = _tile_h(x_ref, xp_ref, xn_ref, g1_ref[...], sc1, sh1)
    h_up, h_dn = _shift_rows(h, h_prev, h_next)
    xg = h + (0.5 * (h_up + h_dn) - h) * mu_wag_ref[2:3, :]

    hb = h.astype(BF16)
    z = _dot(hb, w_in_ref[:, 0:d_pool])
    zg = _dot(hb, w_in_ref[:, d_pool + 3 * d:d_pool + 5 * d])
    outs = []
    for g in range(len(POOL_WINDOWS)):
        zp = z[:, g * POOL_GROUP:(g + 1) * POOL_GROUP]
        zh, zl = _split2(zp)
        band = band_ref[g]
        mixed = (_dot(band, zh) + _dot(band, zl)) * icnt_ref[g] - zp
        outs.append(_dot(mixed.astype(BF16), pool_w_ref[g]))
    a_out = jnp.concatenate(outs, axis=-1) * pool_scale_ref[...]
    gate_a = _sigmoid(zg[:, 0:d])
    gate_b = _sigmoid(zg[:, d:2 * d])
    gl = _sigmoid(_dot(xg.astype(BF16), gw1_ref[...]))
    g_out = _dot(gl.astype(BF16), gw2_ref[...])

    seg = seg_ref[...]
    y = jnp.concatenate([y_ref[q] for q in range(y_ref.shape[0])], axis=-1)
    yc = y - _seg_sum(y, seg) * (1.0 / HEAD_DIM)
    var = _seg_sum(yc * yc, seg) * (1.0 / HEAD_DIM)
    yn = yc * lax.rsqrt(var + LNX_EPS) * lnw_ref[...] + lnb_ref[...]
    b_out = (yn + bonus_ref[...]) * g_out
    mix = _dot((gate_a * a_out + gate_b * b_out).astype(BF16), w_out_ref[...])
    x = x + ga1 * mix

    h2 = _modulated_norm(x, g2_ref[...], sc2, sh2)
    u = jnp.maximum(_dot(h2.astype(BF16), w_ff1_ref[...]), 0.0)
    ff = _dot((u * u).astype(BF16), w_ff2_ref[...])
    x = x + ga2 * ff
    if final_norm:
        x = x * lax.rsqrt(jnp.mean(x * x, axis=-1, keepdims=True) + RMS_EPS) * gf_ref[...]
    o_ref[...] = x


def _post_call(x, mod, y, bonus, p, consts, g_final, final_norm):
    bsz, seq, d = x.shape
    tm = TOKEN_TILE
    x_spec, xp_spec, xn_spec = _token_specs(tm, seq, d)
    mod_spec = pl.BlockSpec((None, 1, mod.shape[-1]), lambda b, j: (b % mod.shape[0], 0, 0))
    weights = (p["g1"], p["g2"], g_final, p["w_in"], p["mu_wag"], p["gw1"], p["gw2"], p["ln_x_w"],
               p["ln_x_b"], p["pool_w"], p["pool_scale"], consts["band"], consts["icnt"], consts["seg"],
               p["w_out"], p["w_ff1"], p["w_ff2"])
    return pl.pallas_call(
        functools.partial(_post_kernel, final_norm=final_norm),
        out_shape=jax.ShapeDtypeStruct((bsz, seq, d), F32),
        grid=(bsz, seq // tm),
        in_specs=[x_spec, xp_spec, xn_spec, mod_spec, _pair_spec(tm, d // PAIR), x_spec]
        + [_const_spec(w.shape) for w in weights],
        out_specs=x_spec,
        compiler_params=pltpu.CompilerParams(dimension_semantics=("parallel", "parallel"),
                                             vmem_limit_bytes=VMEM_LIMIT),
        name="post_scan",
    )(x, x, x, mod, y, bonus, *weights)


def _pool_consts(tm, row_len):
    t = np.arange(tm)
    pos, row = t % row_len, t // row_len
    bands, icnts = [], []
    for win in POOL_WINDOWS:
        lo = np.clip(pos - win // 2, 0, row_len)
        hi = np.clip(pos + win - win // 2, 0, row_len)
        inside = (pos[None, :] >= lo[:, None]) & (pos[None, :] < hi[:, None]) & (row[None, :] == row[:, None])
        bands.append(inside.astype(np.float32))
        icnts.append(np.broadcast_to((1.0 / (hi - lo))[:, None], (tm, POOL_GROUP)).astype(np.float32))
    return jnp.asarray(np.stack(bands), BF16), jnp.asarray(np.stack(icnts), F32)


def _block_diag2(w):
    z = jnp.zeros_like(w[0])
    return jnp.concatenate([jnp.concatenate([w[0], z], axis=1), jnp.concatenate([z, w[1]], axis=1)], axis=0)


def _layer_params(li, w):
    row = lambda a: a.reshape(1, -1)
    return {
        "g1": row(w["g_norm1"][li]), "g2": row(w["g_norm2"][li]),
        "w_in": w["w_in"][li].astype(BF16),
        "mu_rkv": row(w["mu_rkv"][li]), "mu_wag": w["mu_wag"][li],
        "wd1": jnp.concatenate([w["w_dec1"][li, 0], w["w_dec1"][li, 1]], axis=1).astype(BF16),
        "wd2": _block_diag2(w["w_dec2"][li]).astype(BF16), "wd0": row(w["w_dec0"][li]),
        "a1": jnp.concatenate([w["a1"][li, 0], w["a1"][li, 1]], axis=1).astype(BF16),
        "a2": _block_diag2(w["a2"][li]).astype(BF16), "a0": row(w["a0"][li]),
        "gw1": w["gate_w1"][li].astype(BF16), "gw2": w["gate_w2"][li].astype(BF16),
        "k_k": row(w["k_k"][li]), "k_a": row(w["k_a"][li]), "r_k": row(w["r_k"][li]),
        "ln_x_w": row(w["ln_x_w"][li]), "ln_x_b": row(w["ln_x_b"][li]),
        "pool_w": w["pool_w"][li].astype(BF16), "pool_scale": row(w["pool_scale"][li]),
        "w_out": w["w_out"][li].astype(BF16), "w_ff1": w["w_ff1"][li].astype(BF16),
        "w_ff2": w["w_ff2"][li].astype(BF16),
    }


def _layer(x, mod, s0, want_final, p, consts, g_final, final_norm):
    ops = _pre_call(x, mod, p, consts)
    y, s_fin = _scan_call(ops[:9], s0, want_final)
    return _post_call(x, mod, y, ops[9], p, consts, g_final, final_norm), s_fin


def kernel(x_prompt, x_sample, state_rwkv, c, c_ctx, w_ada, b_ada, g_norm1, g_norm2, w_in, mu_rkv, mu_wag, w_dec0, w_dec1, w_dec2, a0, a1, a2, gate_w1, gate_w2, k_k, k_a, r_k, ln_x_w, ln_x_b, pool_w, pool_scale, w_out, w_ff1, w_ff2, g_final):
    w = dict(w_ada=w_ada, b_ada=b_ada, g_norm1=g_norm1, g_norm2=g_norm2, w_in=w_in, mu_rkv=mu_rkv,
             mu_wag=mu_wag, w_dec0=w_dec0, w_dec1=w_dec1, w_dec2=w_dec2, a0=a0, a1=a1, a2=a2,
             gate_w1=gate_w1, gate_w2=gate_w2, k_k=k_k, k_a=k_a, r_k=r_k, ln_x_w=ln_x_w, ln_x_b=ln_x_b,
             pool_w=pool_w, pool_scale=pool_scale, w_out=w_out, w_ff1=w_ff1, w_ff2=w_ff2)
    depth = w_in.shape[0]
    n_lat, lat_len, d = x_sample.shape
    ctx_len = x_prompt.shape[1]
    assert ctx_len % TOKEN_TILE == 0 and lat_len % TOKEN_TILE == 0 and TOKEN_TILE % GRID_W == 0
    assert ctx_len == TOKEN_TILE, "context pooling runs over the whole sequence inside one token tile"

    seg = jnp.asarray(np.kron(np.eye(SEG_TILE // HEAD_DIM), np.ones((HEAD_DIM, HEAD_DIM))), BF16)
    band_c, icnt_c = _pool_consts(TOKEN_TILE, ctx_len)
    band_l, icnt_l = _pool_consts(TOKEN_TILE, GRID_W)
    consts_ctx = {"seg": seg, "band": band_c, "icnt": icnt_c}
    consts_lat = {"seg": seg, "band": band_l, "icnt": icnt_l}

    n_cv = 1 + n_lat
    cv = jnp.concatenate([c_ctx[None, :], c, jnp.zeros((-n_cv % SUBLANES, d), F32)], axis=0)
    g_fin = g_final.reshape(1, d)
    ctx, lat = x_prompt, x_sample
    new_states = []
    for li in range(depth):
        p = _layer_params(li, w)
        mod = _ada_call(cv, w_ada[li], b_ada[li].reshape(1, -1))
        mod_ctx = mod[0:1].reshape(1, 1, -1)
        mod_lat = mod[1:n_cv].reshape(n_lat, 1, -1)
        last = li == depth - 1
        ctx, s_fin = _layer(ctx, mod_ctx, None, True, p, consts_ctx, g_fin, last)
        new_states.append(s_fin)
        lat, _ = _layer(lat, mod_lat, state_rwkv[:, li], False, p, consts_lat, g_fin, last)
    return ctx, lat, jnp.stack(new_states, axis=1)
```

```python
import functools

import numpy as np
import jax
import jax.numpy as jnp
from jax import lax
from jax.experimental import pallas as pl
from jax.experimental.pallas import tpu as pltpu

F32 = jnp.float32
BF16 = jnp.bfloat16

N_HEADS = 16
HEAD_DIM = 64
PAIR = 2 * HEAD_DIM
POOL_WINDOWS = (2, 4, 8, 16)
POOL_GROUP = 128
GRID_W = 64
N_MOD = 6
RMS_EPS = 1e-6
LNX_EPS = 64e-5
NORM_EPS = 1e-12

CHUNK = 64
SCAN_OPERAND_DTYPES = (F32, F32, BF16, F32, BF16, F32, F32, F32)
BASE_BLOCK = 8
ITEMS_PER_TRIP = 8
SCAN_VMEM_BUDGET = 50 * 1024 * 1024
TOKEN_TILE = 256
SEG_TILE = 256
SUBLANES = 8
VMEM_LIMIT = 58 * 1024 * 1024


def _dot(a, b):
    return jnp.dot(a, b, preferred_element_type=F32)


def _dot_nt(a, b):
    return lax.dot_general(a, b, (((1,), (1,)), ((), ())), preferred_element_type=F32)


def _dot_tn(a, b):
    return lax.dot_general(a, b, (((0,), (0,)), ((), ())), preferred_element_type=F32)


def _dot_f32(a, b):
    return jnp.dot(a, b, precision=lax.Precision.HIGHEST, preferred_element_type=F32)


def _split2(x):
    hi = x.astype(BF16)
    return hi, (x - hi.astype(F32)).astype(BF16)


def _dot_exact_rhs(x, m):
    hi, lo = _split2(x)
    return _dot(hi, m) + _dot(lo, m)


def _dot_exact_lhs(m, x):
    hi, lo = _split2(x)
    return _dot(m, hi) + _dot(m, lo)


def _seg_sum(x, seg_ones):
    d = x.shape[-1]
    parts = [_dot_exact_rhs(x[:, q:q + SEG_TILE], seg_ones) for q in range(0, d, SEG_TILE)]
    return jnp.concatenate(parts, axis=-1)


def _sigmoid(x):
    return 1.0 / (1.0 + jnp.exp(-x))


def _modulated_norm(x, g, scale, shift):
    y = x * lax.rsqrt(jnp.mean(x * x, axis=-1, keepdims=True) + RMS_EPS)
    return (y * g) * (1.0 + scale) + shift


def _shift_rows(v, first_row, last_row):
    n = v.shape[0]
    row = lax.broadcasted_iota(jnp.int32, v.shape, 0)
    up = jnp.where(row == 0, first_row, pltpu.roll(v, 1, 0))
    dn = jnp.where(row == n - 1, last_row, pltpu.roll(v, n - 1, 0))
    return up, dn


def _tile_h(x_ref, xp_ref, xn_ref, g, scale, shift):
    j = pl.program_id(1)
    nj = pl.num_programs(1)
    h = _modulated_norm(x_ref[...], g, scale, shift)
    halo = jnp.concatenate([xp_ref[SUBLANES - 1:SUBLANES, :], xn_ref[0:1, :]], axis=0)
    hh = _modulated_norm(halo, g, scale, shift)
    h_prev = jnp.where(j > 0, hh[0:1, :], 0.0)
    h_next = jnp.where(j < nj - 1, hh[1:2, :], 0.0)
    return h, h_prev, h_next


def _ada_kernel(cv_ref, w_ref, b_ref, o_ref):
    cv = cv_ref[...]
    o_ref[...] = _dot_f32(cv * _sigmoid(cv), w_ref[...]) + b_ref[...]


def _ada_call(cv, w_ada, b_ada):
    rows, d = cv.shape
    n = w_ada.shape[1]
    bn = n // 4
    return pl.pallas_call(
        _ada_kernel,
        out_shape=jax.ShapeDtypeStruct((rows, n), F32),
        grid=(n // bn,),
        in_specs=[pl.BlockSpec((rows, d), lambda i: (0, 0)),
                  pl.BlockSpec((d, bn), lambda i: (0, i)),
                  pl.BlockSpec((1, bn), lambda i: (0, i))],
        out_specs=pl.BlockSpec((rows, bn), lambda i: (0, i)),
        compiler_params=pltpu.CompilerParams(dimension_semantics=("arbitrary",),
                                             vmem_limit_bytes=VMEM_LIMIT),
        name="ada_mod",
    )(cv, w_ada, b_ada)


def _pre_kernel(x_ref, xp_ref, xn_ref, mod_ref, g1_ref, w_in_ref, mu_rkv_ref, mu_wag_ref,
                wd1_ref, wd2_ref, wd0_ref, a1_ref, a2_ref, a0_ref, kk_w_ref, rk_ref,
                seg_ref,
                pw0_ref, pw1_ref, r_ref, k_ref, v_ref, kk_ref, al0_ref, al1_ref,
                bonus_ref):
    d = x_ref.shape[-1]
    mod = mod_ref[...]
    shift, scale = mod[:, 0:d], mod[:, d:2 * d]
    h, h_prev, h_next = _tile_h(x_ref, xp_ref, xn_ref, g1_ref[...], scale, shift)
    h_up, h_dn = _shift_rows(h, h_prev, h_next)
    hd = 0.5 * (h_up + h_dn) - h

    d_pool = len(POOL_WINDOWS) * POOL_GROUP
    w_rkv = w_in_ref[:, d_pool:d_pool + 3 * d]
    z = _dot(h.astype(BF16), w_rkv)
    halo = jnp.concatenate([h_prev, h_next, jnp.zeros((SUBLANES - 2, d), F32)], axis=0)
    zh = _dot(halo.astype(BF16), w_rkv)
    z_up, z_dn = _shift_rows(z, zh[0:1, :], zh[1:2, :])
    mu_rkv = mu_rkv_ref[...]
    z = z * (1.0 - mu_rkv) + (z_up + z_dn) * (0.5 * mu_rkv)
    r, k, v = z[:, 0:d], z[:, d:2 * d], z[:, 2 * d:3 * d]

    mu = mu_wag_ref[...]
    xw = h + hd * mu[0:1, :]
    xa = h + hd * mu[1:2, :]

    tw = jnp.tanh(_dot(xw.astype(BF16), wd1_ref[...]))
    pre_w = wd0_ref[...] + _dot(tw.astype(BF16), wd2_ref[...])
    ta = _dot(xa.astype(BF16), a1_ref[...])
    a_logit = a0_ref[...] + _dot(ta.astype(BF16), a2_ref[...])

    seg = seg_ref[...]
    kkr = k * kk_w_ref[...]
    kk = kkr * lax.rsqrt(_seg_sum(kkr * kkr, seg) + NORM_EPS)

    _store_pairs(pw0_ref, pre_w[:, 0:d])
    _store_pairs(pw1_ref, pre_w[:, d:2 * d])
    _store_pairs(r_ref, r)
    _store_pairs(k_ref, k)
    _store_pairs(v_ref, v)
    _store_pairs(kk_ref, kk)
    _store_pairs(al0_ref, a_logit[:, 0:d])
    _store_pairs(al1_ref, a_logit[:, d:2 * d])
    bonus_ref[...] = _seg_sum(r * k * rk_ref[...], seg) * v


def _store_pairs(ref, val):
    for q in range(ref.shape[0]):
        ref[q] = val[:, q * PAIR:(q + 1) * PAIR].astype(ref.dtype)


def _pair_spec(tm, n_pairs):
    return pl.BlockSpec((None, n_pairs, tm, PAIR), lambda b, j: (b, 0, j, 0))


def _const_spec(shape):
    nd = len(shape)
    return pl.BlockSpec(shape, lambda b, j: (0,) * nd, pipeline_mode=pl.Buffered(1))


def _token_specs(tm, seq, d):
    tiles8 = tm // SUBLANES
    last8 = seq // SUBLANES - 1
    x_spec = pl.BlockSpec((None, tm, d), lambda b, j: (b, j, 0))
    xp_spec = pl.BlockSpec((None, SUBLANES, d), lambda b, j: (b, jnp.maximum(j * tiles8 - 1, 0), 0))
    xn_spec = pl.BlockSpec((None, SUBLANES, d), lambda b, j: (b, jnp.minimum((j + 1) * tiles8, last8), 0))
    return x_spec, xp_spec, xn_spec


def _pre_call(x, mod, p, consts):
    bsz, seq, d = x.shape
    tm = TOKEN_TILE
    x_spec, xp_spec, xn_spec = _token_specs(tm, seq, d)
    mod_spec = pl.BlockSpec((None, 1, mod.shape[-1]), lambda b, j: (b % mod.shape[0], 0, 0))
    weights = (p["g1"], p["w_in"], p["mu_rkv"], p["mu_wag"], p["wd1"], p["wd2"], p["wd0"],
               p["a1"], p["a2"], p["a0"], p["k_k"], p["r_k"], consts["seg"])
    n_pairs = d // PAIR
    pm = lambda dt: jax.ShapeDtypeStruct((bsz, n_pairs, seq, PAIR), dt)
    return pl.pallas_call(
        _pre_kernel,
        out_shape=tuple(pm(dt) for dt in SCAN_OPERAND_DTYPES) + (jax.ShapeDtypeStruct((bsz, seq, d), F32),),
        grid=(bsz, seq // tm),
        in_specs=[x_spec, xp_spec, xn_spec, mod_spec] + [_const_spec(w.shape) for w in weights],
        out_specs=(_pair_spec(tm, n_pairs),) * len(SCAN_OPERAND_DTYPES) + (x_spec,),
        compiler_params=pltpu.CompilerParams(dimension_semantics=("parallel", "parallel"),
                                             vmem_limit_bytes=VMEM_LIMIT),
        name="pre_scan",
    )(x, x, x, mod, *weights)


def _stack(x, lane_head):
    zero = jnp.zeros_like(x)
    return jnp.concatenate([jnp.where(lane_head == 0, x, zero), jnp.where(lane_head == 1, x, zero)], axis=0)


def _chunk_prep(insts, cst):
    lane_head, bd = cst["lane_head"], cst["bd"]
    c = insts[0][0].shape[0]
    ids = range(len(insts))
    rev = [t[7] for t in insts]
    stk = lambda x: _stack(x, lane_head)
    lws = [-_sigmoid(t[0]) * float(np.exp(-0.5)) for t in insts]
    cum = [_dot_exact_lhs(cst["tri"][rev[i]], lws[i]) for i in ids]
    tot = [cum[i][(0 if rev[i] else c - 1):(1 if rev[i] else c), :] for i in ids]
    yield
    op = []
    for i in ids:
        r, k, v, kk = (t.astype(F32) for t in insts[i][1:5])
        lw = lws[i]
        a_gate = _sigmoid(insts[i][5])
        kd = k * (1.0 + (a_gate - 1.0) * insts[i][6])
        b = kk * a_gate
        e_in = jnp.exp(cum[i])
        e_ex = jnp.exp(cum[i] - lw)
        e_ng = jnp.exp(-cum[i])
        e_end = jnp.exp(tot[i] - cum[i])
        a = kk * e_ex
        op.append(dict(
            r=r * e_in,
            ar=jnp.concatenate([a, r * e_in], axis=0).astype(BF16),
            bk=jnp.concatenate([stk(b * e_ng), stk(kd * e_ng)], axis=0).astype(BF16),
            a_st=stk(a).astype(BF16), v_st=stk(v).astype(BF16), v=v.astype(BF16),
            bh=(-(b * e_end)).astype(BF16), bkh=jnp.concatenate([-(b * e_end), kd * e_end], axis=0).astype(BF16)))
    sc = [_dot_nt(op[i]["ar"], op[i]["bk"]) for i in ids]
    yield
    strict = [cst["strict"][rev[i]] for i in ids]
    incl = [cst["incl"][rev[i]] for i in ids]
    l_w = [jnp.where(strict[i], -sc[i][:c, :PAIR], 0.0) for i in ids]
    m_kk = [jnp.concatenate([jnp.where(strict[i], sc[i][:c, PAIR:], 0.0),
                             jnp.where(incl[i], sc[i][c:, PAIR:], 0.0)], axis=0).astype(BF16) for i in ids]
    m_rb = [jnp.where(incl[i], -sc[i][c:, :PAIR], 0.0).astype(BF16) for i in ids]
    mvk = [_dot(m_kk[i], op[i]["v_st"]) for i in ids]
    yield

    x_w = []
    for i in ids:
        d_w = jnp.where(cst["blk"][BASE_BLOCK], l_w[i], 0.0)
        x_w.append((cst["eye_w"] + d_w, d_w.astype(BF16)))
    p_w = [_dot(x_w[i][1], stk(x_w[i][1])) for i in ids]
    x_w = [x_w[i][0] for i in ids]
    yield
    n_sq = int(np.log2(BASE_BLOCK)) - 1
    for q in range(n_sq):
        p_bd = [stk(p_w[i].astype(BF16)) for i in ids]
        if q < n_sq - 1:
            zz = [_dot(jnp.concatenate([x_w[i], p_w[i]], axis=0).astype(BF16), p_bd[i]) for i in ids]
            x_w = [x_w[i] + zz[i][:c] for i in ids]
            p_w = [zz[i][c:] for i in ids]
        else:
            x_w = [x_w[i] + _dot(x_w[i].astype(BF16), p_bd[i]) for i in ids]
        yield
    m = BASE_BLOCK
    while m < c:
        f_bd = [stk(jnp.where(cst["blk"][2 * m] & ~cst["blk"][m], l_w[i], 0.0).astype(BF16)) for i in ids]
        t_b = [x_w[i].astype(BF16) for i in ids]
        u_w = [_dot(t_b[i], f_bd[i]) for i in ids]
        yield
        x_w = [x_w[i] + _dot(u_w[i].astype(BF16), stk(t_b[i])) for i in ids]
        yield
        m *= 2

    w_p = [_dot(x_w[i].astype(BF16), jnp.concatenate([op[i]["a_st"], stk(mvk[i][:c]).astype(BF16)], axis=1))
           for i in ids]
    yield
    w_st = [jnp.concatenate([stk(w_p[i][:, :PAIR]), stk(w_p[i][:, PAIR:])], axis=1).astype(BF16) for i in ids]
    ry = [_dot(m_rb[i], w_st[i]) for i in ids]
    g_f = [_dot_tn(op[i]["bh"], w_p[i][:, :PAIR].astype(BF16)) for i in ids]
    uv = [jnp.concatenate([w_p[i][:, PAIR:].astype(BF16), op[i]["v"]], axis=0) for i in ids]
    h_f = [_dot_tn(op[i]["bkh"], uv[i]) for i in ids]
    yield
    out = []
    for i in ids:
        r_p = op[i]["r"] + ry[i][:, :PAIR]
        g_m = jnp.where(cst["eye"], jnp.exp(tot[i]), 0.0) + jnp.where(bd, g_f[i], 0.0)
        h_bd = jnp.where(bd, h_f[i], 0.0)
        out.append((jnp.concatenate([r_p, g_m], axis=0).astype(BF16), ry[i][:, PAIR:] + mvk[i][c:],
                    h_bd[:c] + h_bd[c:]))
    return out


def _interleave(*gens):
    results = [None] * len(gens)
    live = dict(enumerate(gens))
    while live:
        for i in list(live):
            try:
                next(live[i])
            except StopIteration as done:
                results[i] = done.value
                del live[i]
    return results


def _scan_consts(c):
    n = 2 * c
    row = lax.broadcasted_iota(jnp.int32, (n, n), 0)
    col = lax.broadcasted_iota(jnp.int32, (n, n), 1)
    tr = lax.broadcasted_iota(jnp.int32, (c, c), 0)
    tc = lax.broadcasted_iota(jnp.int32, (c, c), 1)
    t = lax.broadcasted_iota(jnp.int32, (c, PAIR), 0)
    lane = lax.broadcasted_iota(jnp.int32, (c, PAIR), 1)
    s = lane % HEAD_DIM
    return {
        "tri": (jnp.where(tc <= tr, 1.0, 0.0).astype(BF16), jnp.where(tc >= tr, 1.0, 0.0).astype(BF16)),
        "strict": (s < t, s > t),
        "incl": (s <= t, s >= t),
        "eye_w": jnp.where(s == t, 1.0, 0.0),
        "blk": {m: (s // m) == (t // m) for m in (BASE_BLOCK << e for e in range(int(np.log2(c // BASE_BLOCK)) + 1))},
        "eye": row == col,
        "bd": (row // c) == (col // c),
        "lane_head": lane // HEAD_DIM,
    }


def _scan_kernel(*refs, has_s0, want_final):
    pw0_ref, pw1_ref, r_ref, k_ref, v_ref, kk_ref, al0_ref, al1_ref, ka_ref = refs[:9]
    pos = 9
    s0_ref = None
    if has_s0:
        s0_ref = refs[pos]
        pos += 1
    y_ref = refs[pos]
    pos += 1
    sf_ref = None
    if want_final:
        sf_ref = refs[pos]
        pos += 1
    yb_ref, st_ref, rg_ref, h_ref = refs[pos:pos + 4]

    c = CHUNK
    pp, seq = r_ref.shape[0], r_ref.shape[1]
    nc = seq // c
    ipt = min(ITEMS_PER_TRIP, pp * nc)
    cst = _scan_consts(c)
    zero = jnp.zeros((HEAD_DIM, HEAD_DIM), F32)
    pw_refs, al_refs, yv_refs = (pw0_ref, pw1_ref), (al0_ref, al1_ref), (y_ref, yb_ref)

    for q in range(pp):
        for dr in range(2):
            if has_s0:
                s_a = s0_ref[dr, 2 * q].T
                s_b = s0_ref[dr, 2 * q + 1].T
                st_ref[q, dr] = jnp.concatenate([jnp.concatenate([s_a, zero], axis=1),
                                                 jnp.concatenate([zero, s_b], axis=1)], axis=0)
            else:
                st_ref[q, dr] = jnp.zeros((PAIR, PAIR), F32)

    cps = min(nc, ipt)
    ppt = ipt // cps
    tpp = nc // cps
    n_trips = (pp * nc) // ipt

    def trip_chunks(t):
        g, k = t // tpp, t % tpp
        return [[(g * ppt + u, dr, (k * cps + j) if dr == 0 else nc - 1 - (k * cps + j))
                 for u in range(ppt) for dr in range(2)] for j in range(cps)]

    def prep(t):
        dest = [x for step in trip_chunks(t) for x in step]
        insts = []
        for q, dr, ck in dest:
            rows = pl.ds(pl.multiple_of(ck * c, c), c)
            insts.append((pw_refs[dr][q, rows, :], r_ref[q, rows, :], k_ref[q, rows, :], v_ref[q, rows, :],
                          kk_ref[q, rows, :], al_refs[dr][q, rows, :], ka_ref[q], dr))
        outs = yield from _chunk_prep(insts, cst)
        for (q, dr, ck), (rg, y_v, h_m) in zip(dest, outs):
            rg_ref[q, dr, ck] = rg
            h_ref[q, dr, ck] = h_m
            yv_refs[dr][q, pl.ds(pl.multiple_of(ck * c, c), c), :] = y_v

    def chain(t):
        for step in trip_chunks(t):
            z = [_dot(rg_ref[q, dr, ck], st_ref[q, dr].astype(BF16)) for q, dr, ck in step]
            for (q, dr, ck), z_i in zip(step, z):
                rows = pl.ds(pl.multiple_of(ck * c, c), c)
                yv_refs[dr][q, rows, :] = yv_refs[dr][q, rows, :] + z_i[:c]
                h_w = h_ref[q, dr, ck]
                st_ref[q, dr] = z_i[c:] + jnp.where(cst["bd"], jnp.concatenate([h_w, h_w], axis=0), 0.0)
            yield

    _interleave(prep(0))

    def trip_body(t, carry):
        _interleave(chain(t - 1), prep(t))
        return carry

    lax.fori_loop(1, n_trips, trip_body, 0)
    _interleave(chain(n_trips - 1))
    y_ref[...] = y_ref[...] + yb_ref[...]
    if want_final:
        for q in range(pp):
            for dr in range(2):
                s_bd = st_ref[q, dr]
                sf_ref[dr, 2 * q] = s_bd[:HEAD_DIM, :HEAD_DIM].T
                sf_ref[dr, 2 * q + 1] = s_bd[HEAD_DIM:, HEAD_DIM:].T


def _pairs_per_step(n_pairs, seq, ops):
    nc = seq // CHUNK
    io = (sum(o.dtype.itemsize for o in ops) + 4) * 2 * seq * PAIR
    scratch = 4 * seq * PAIR + 2 * nc * ((CHUNK + PAIR) * PAIR * 2 + CHUNK * PAIR * 4) + 2 * PAIR * PAIR * 4
    pp = 1
    while pp * 2 <= n_pairs and n_pairs % (pp * 2) == 0 and (pp * 2) * (io + scratch) <= SCAN_VMEM_BUDGET:
        pp *= 2
    return pp


def _scan_call(ops, k_a, s0, want_final):
    bsz, n_pairs, seq, _ = ops[0].shape
    nc = seq // CHUNK
    pp = _pairs_per_step(n_pairs, seq, ops)
    ipt = min(ITEMS_PER_TRIP, pp * nc)
    assert (pp * nc) % ipt == 0 and ipt % min(nc, ipt) == 0 and nc % min(nc, ipt) == 0
    col_spec = pl.BlockSpec((None, pp, seq, PAIR), lambda b, g: (b, g, 0, 0))
    st_spec = pl.BlockSpec((None, 2, 2 * pp, HEAD_DIM, HEAD_DIM), lambda b, g: (b, 0, g, 0, 0))
    in_specs = [col_spec] * len(ops) + [pl.BlockSpec((pp, 1, PAIR), lambda b, g: (g, 0, 0))]
    args = list(ops) + [k_a]
    has_s0 = s0 is not None
    if has_s0:
        in_specs.append(st_spec)
        args.append(s0)
    out_shape = [jax.ShapeDtypeStruct((bsz, n_pairs, seq, PAIR), F32)]
    out_specs = [col_spec]
    if want_final:
        out_shape.append(jax.ShapeDtypeStruct((bsz, 2, N_HEADS, HEAD_DIM, HEAD_DIM), F32))
        out_specs.append(st_spec)
    res = pl.pallas_call(
        functools.partial(_scan_kernel, has_s0=has_s0, want_final=want_final),
        out_shape=tuple(out_shape),
        grid=(bsz, n_pairs // pp),
        in_specs=in_specs,
        out_specs=tuple(out_specs),
        scratch_shapes=[pltpu.VMEM((pp, seq, PAIR), F32), pltpu.VMEM((pp, 2, PAIR, PAIR), F32),
                        pltpu.VMEM((pp, 2, nc, CHUNK + PAIR, PAIR), BF16), pltpu.VMEM((pp, 2, nc, CHUNK, PAIR), F32)],
        compiler_params=pltpu.CompilerParams(dimension_semantics=("parallel", "parallel"),
                                             vmem_limit_bytes=VMEM_LIMIT),
        name="wkv_scan",
    )(*args)
    return res if want_final else (res[0], None)


def _post_kernel(x_ref, xp_ref, xn_ref, mod_ref, y_ref, bonus_ref, g1_ref, g2_ref, gf_ref,
                 w_in_ref, mu_wag_ref, gw1_ref, gw2_ref, lnw_ref, lnb_ref, pool_w_ref, pool_scale_ref,
                 band_ref, icnt_ref, seg_ref, w_out_ref, w_ff1_ref, w_ff2_ref, o_ref, *, final_norm):
    d = x_ref.shape[-1]
    d_pool = len(POOL_WINDOWS) * POOL_GROUP
    mod = mod_ref[...]
    sh1, sc1, ga1, sh2, sc2, ga2 = (mod[:, i * d:(i + 1) * d] for i in range(N_MOD))
    x = x_ref[...]
    h, h_prev, h_next = _tile_h(x_ref, xp_ref, xn_ref, g1_ref[...], sc1, sh1)
    h_up, h_dn = _shift_rows(h, h_prev, h_next)
    xg = h + (0.5 * (h_up + h_dn) - h) * mu_wag_ref[2:3, :]

    hb = h.astype(BF16)
    z = _dot(hb, w_in_ref[:, 0:d_pool])
    zg = _dot(hb, w_in_ref[:, d_pool + 3 * d:d_pool + 5 * d])
    outs = []
    for g in range(len(POOL_WINDOWS)):
        zp = z[:, g * POOL_GROUP:(g + 1) * POOL_GROUP]
        zh, zl = _split2(zp)
        band = band_ref[g]
        mixed = (_dot(band, zh) + _dot(band, zl)) * icnt_ref[g] - zp
        outs.append(_dot(mixed.astype(BF16), pool_w_ref[g]))
    a_out = jnp.concatenate(outs, axis=-1) * pool_scale_ref[...]
    gate_a = _sigmoid(zg[:, 0:d])
    gate_b = _sigmoid(zg[:, d:2 * d])
    gl = _sigmoid(_dot(xg.astype(BF16), gw1_ref[...]))
    g_out = _dot(gl.astype(BF16), gw2_ref[...])

    seg = seg_ref[...]
    y = jnp.concatenate([y_ref[q] for q in range(y_ref.shape[0])], axis=-1)
    yc = y - _seg_sum(y, seg) * (1.0 / HEAD_DIM)
    var = _seg_sum(yc * yc, seg) * (1.0 / HEAD_DIM)
    yn = yc * lax.rsqrt(var + LNX_EPS) * lnw_ref[...] + lnb_ref[...]
    b_out = (yn + bonus_ref[...]) * g_out
    mix = _dot((gate_a * a_out + gate_b * b_out).astype(BF16), w_out_ref[...])
    x = x + ga1 * mix

    h2 = _modulated_norm(x, g2_ref[...], sc2, sh2)
    u = jnp.maximum(_dot(h2.astype(BF16), w_ff1_ref[...]), 0.0)
    ff = _dot((u * u).astype(BF16), w_ff2_ref[...])
    x = x + ga2 * ff
    if final_norm:
        x = x * lax.rsqrt(jnp.mean(x * x, axis=-1, keepdims=True) + RMS_EPS) * gf_ref[...]
    o_ref[...] = x


def _post_call(x, mod, y, bonus, p, consts, g_final, final_norm):
    bsz, seq, d = x.shape
    tm = TOKEN_TILE
    x_spec, xp_spec, xn_spec = _token_specs(tm, seq, d)
    mod_spec = pl.BlockSpec((None, 1, mod.shape[-1]), lambda b, j: (b % mod.shape[0], 0, 0))
    weights = (p["g1"], p["g2"], g_final, p["w_in"], p["mu_wag"], p["gw1"], p["gw2"], p["ln_x_w"],
               p["ln_x_b"], p["pool_w"], p["pool_scale"], consts["band"], consts["icnt"], consts["seg"],
               p["w_out"], p["w_ff1"], p["w_ff2"])
    return pl.pallas_call(
        functools.partial(_post_kernel, final_norm=final_norm),
        out_shape=jax.ShapeDtypeStruct((bsz, seq, d), F32),
        grid=(bsz, seq // tm),
        in_specs=[x_spec, xp_spec, xn_spec, mod_spec, _pair_spec(tm, d // PAIR), x_spec]
        + [_const_spec(w.shape) for w in weights],
        out_specs=x_spec,
        compiler_params=pltpu.CompilerParams(dimension_semantics=("parallel", "parallel"),
                                             vmem_limit_bytes=VMEM_LIMIT),
        name="post_scan",
    )(x, x, x, mod, y, bonus, *weights)


def _pool_consts(tm, row_len):
    t = np.arange(tm)
    pos, row = t % row_len, t // row_len
    bands, icnts = [], []
    for win in POOL_WINDOWS:
        lo = np.clip(pos - win // 2, 0, row_len)
        hi = np.clip(pos + win - win // 2, 0, row_len)
        inside = (pos[None, :] >= lo[:, None]) & (pos[None, :] < hi[:, None]) & (row[None, :] == row[:, None])
        bands.append(inside.astype(np.float32))
        icnts.append(np.broadcast_to((1.0 / (hi - lo))[:, None], (tm, POOL_GROUP)).astype(np.float32))
    return jnp.asarray(np.stack(bands), BF16), jnp.asarray(np.stack(icnts), F32)


def _block_diag2(w):
    z = jnp.zeros_like(w[0])
    return jnp.concatenate([jnp.concatenate([w[0], z], axis=1), jnp.concatenate([z, w[1]], axis=1)], axis=0)


def _layer_params(li, w):
    row = lambda a: a.reshape(1, -1)
    return {
        "g1": row(w["g_norm1"][li]), "g2": row(w["g_norm2"][li]),
        "w_in": w["w_in"][li].astype(BF16),
        "mu_rkv": row(w["mu_rkv"][li]), "mu_wag": w["mu_wag"][li],
        "wd1": jnp.concatenate([w["w_dec1"][li, 0], w["w_dec1"][li, 1]], axis=1).astype(BF16),
        "wd2": _block_diag2(w["w_dec2"][li]).astype(BF16), "wd0": row(w["w_dec0"][li]),
        "a1": jnp.concatenate([w["a1"][li, 0], w["a1"][li, 1]], axis=1).astype(BF16),
        "a2": _block_diag2(w["a2"][li]).astype(BF16), "a0": row(w["a0"][li]),
        "gw1": w["gate_w1"][li].astype(BF16), "gw2": w["gate_w2"][li].astype(BF16),
        "k_k": row(w["k_k"][li]), "k_a": row(w["k_a"][li]), "r_k": row(w["r_k"][li]),
        "ln_x_w": row(w["ln_x_w"][li]), "ln_x_b": row(w["ln_x_b"][li]),
        "pool_w": w["pool_w"][li].astype(BF16), "pool_scale": row(w["pool_scale"][li]),
        "w_out": w["w_out"][li].astype(BF16), "w_ff1": w["w_ff1"][li].astype(BF16),
        "w_ff2": w["w_ff2"][li].astype(BF16),
    }


def _layer(x, mod, s0, want_final, p, consts, g_final, final_norm):
    ops = _pre_call(x, mod, p, consts)
    y, s_fin = _scan_call(ops[:-1], p["k_a"].reshape(-1, 1, PAIR), s0, want_final)
    return _post_call(x, mod, y, ops[-1], p, consts, g_final, final_norm), s_fin


def kernel(x_prompt, x_sample, state_rwkv, c, c_ctx, w_ada, b_ada, g_norm1, g_norm2, w_in, mu_rkv, mu_wag, w_dec0, w_dec1, w_dec2, a0, a1, a2, gate_w1, gate_w2, k_k, k_a, r_k, ln_x_w, ln_x_b, pool_w, pool_scale, w_out, w_ff1, w_ff2, g_final):
    w = dict(w_ada=w_ada, b_ada=b_ada, g_norm1=g_norm1, g_norm2=g_norm2, w_in=w_in, mu_rkv=mu_rkv,
             mu_wag=mu_wag, w_dec0=w_dec0, w_dec1=w_dec1, w_dec2=w_dec2, a0=a0, a1=a1, a2=a2,
             gate_w1=gate_w1, gate_w2=gate_w2, k_k=k_k, k_a=k_a, r_k=r_k, ln_x_w=ln_x_w, ln_x_b=ln_x_b,
             pool_w=pool_w, pool_scale=pool_scale, w_out=w_out, w_ff1=w_ff1, w_ff2=w_ff2)
    depth = w_in.shape[0]
    n_lat, lat_len, d = x_sample.shape
    ctx_len = x_prompt.shape[1]
    assert ctx_len % TOKEN_TILE == 0 and lat_len % TOKEN_TILE == 0 and TOKEN_TILE % GRID_W == 0
    assert ctx_len == TOKEN_TILE, "context pooling runs over the whole sequence inside one token tile"

    seg = jnp.asarray(np.kron(np.eye(SEG_TILE // HEAD_DIM), np.ones((HEAD_DIM, HEAD_DIM))), BF16)
    band_c, icnt_c = _pool_consts(TOKEN_TILE, ctx_len)
    band_l, icnt_l = _pool_consts(TOKEN_TILE, GRID_W)
    consts_ctx = {"seg": seg, "band": band_c, "icnt": icnt_c}
    consts_lat = {"seg": seg, "band": band_l, "icnt": icnt_l}

    n_cv = 1 + n_lat
    cv = jnp.concatenate([c_ctx[None, :], c, jnp.zeros((-n_cv % SUBLANES, d), F32)], axis=0)
    g_fin = g_final.reshape(1, d)
    ctx, lat = x_prompt, x_sample
    new_states = []
    for li in range(depth):
        p = _layer_params(li, w)
        mod = _ada_call(cv, w_ada[li], b_ada[li].reshape(1, -1))
        mod_ctx = mod[0:1].reshape(1, 1, -1)
        mod_lat = mod[1:n_cv].reshape(n_lat, 1, -1)
        last = li == depth - 1
        ctx, s_fin = _layer(ctx, mod_ctx, None, True, p, consts_ctx, g_fin, last)
        new_states.append(s_fin)
        lat, _ = _layer(lat, mod_lat, state_rwkv[:, li], False, p, consts_lat, g_fin, last)
    return ctx, lat, jnp.stack(new_states, axis=1)
```

```python
import functools

import numpy as np
import jax
import jax.numpy as jnp
from jax import lax
from jax.experimental import pallas as pl
from jax.experimental.pallas import tpu as pltpu

F32 = jnp.float32
BF16 = jnp.bfloat16

N_HEADS = 16
HEAD_DIM = 64
PAIR = 2 * HEAD_DIM
POOL_WINDOWS = (2, 4, 8, 16)
POOL_GROUP = 128
GRID_W = 64
N_MOD = 6
RMS_EPS = 1e-6
LNX_EPS = 64e-5
NORM_EPS = 1e-12

CHUNK = 64
SCAN_OPERAND_DTYPES = (F32, F32, BF16, F32, BF16, F32, F32, F32)
BASE_BLOCK = 8
ITEMS_PER_TRIP = 8
SCAN_VMEM_BUDGET = 50 * 1024 * 1024
TOKEN_TILE = 256
SEG_TILE = 256
SUBLANES = 8
VMEM_LIMIT = 58 * 1024 * 1024


def _dot(a, b):
    return jnp.dot(a, b, preferred_element_type=F32)


def _dot_nt(a, b):
    return lax.dot_general(a, b, (((1,), (1,)), ((), ())), preferred_element_type=F32)


def _dot_tn(a, b):
    return lax.dot_general(a, b, (((0,), (0,)), ((), ())), preferred_element_type=F32)


def _dot_f32(a, b):
    return jnp.dot(a, b, precision=lax.Precision.HIGHEST, preferred_element_type=F32)


def _split2(x):
    hi = x.astype(BF16)
    return hi, (x - hi.astype(F32)).astype(BF16)


def _dot_exact_rhs(x, m):
    hi, lo = _split2(x)
    return _dot(hi, m) + _dot(lo, m)


def _dot_exact_lhs(m, x):
    hi, lo = _split2(x)
    return _dot(m, hi) + _dot(m, lo)


def _seg_sum(x, seg_ones):
    d = x.shape[-1]
    parts = [_dot_exact_rhs(x[:, q:q + SEG_TILE], seg_ones) for q in range(0, d, SEG_TILE)]
    return jnp.concatenate(parts, axis=-1)


def _sigmoid(x):
    return 1.0 / (1.0 + jnp.exp(-x))


def _modulated_norm(x, g, scale, shift):
    y = x * lax.rsqrt(jnp.mean(x * x, axis=-1, keepdims=True) + RMS_EPS)
    return (y * g) * (1.0 + scale) + shift


def _shift_rows(v, first_row, last_row):
    n = v.shape[0]
    row = lax.broadcasted_iota(jnp.int32, v.shape, 0)
    up = jnp.where(row == 0, first_row, pltpu.roll(v, 1, 0))
    dn = jnp.where(row == n - 1, last_row, pltpu.roll(v, n - 1, 0))
    return up, dn


def _tile_h(x_ref, xp_ref, xn_ref, g, scale, shift):
    j = pl.program_id(1)
    nj = pl.num_programs(1)
    h = _modulated_norm(x_ref[...], g, scale, shift)
    halo = jnp.concatenate([xp_ref[SUBLANES - 1:SUBLANES, :], xn_ref[0:1, :]], axis=0)
    hh = _modulated_norm(halo, g, scale, shift)
    h_prev = jnp.where(j > 0, hh[0:1, :], 0.0)
    h_next = jnp.where(j < nj - 1, hh[1:2, :], 0.0)
    return h, h_prev, h_next


def _ada_kernel(cv_ref, w_ref, b_ref, o_ref):
    cv = cv_ref[...]
    o_ref[...] = _dot_f32(cv * _sigmoid(cv), w_ref[...]) + b_ref[...]


def _ada_call(cv, w_ada, b_ada):
    rows, d = cv.shape
    n = w_ada.shape[1]
    bn = n // 4
    return pl.pallas_call(
        _ada_kernel,
        out_shape=jax.ShapeDtypeStruct((rows, n), F32),
        grid=(n // bn,),
        in_specs=[pl.BlockSpec((rows, d), lambda i: (0, 0)),
                  pl.BlockSpec((d, bn), lambda i: (0, i)),
                  pl.BlockSpec((1, bn), lambda i: (0, i))],
        out_specs=pl.BlockSpec((rows, bn), lambda i: (0, i)),
        compiler_params=pltpu.CompilerParams(dimension_semantics=("arbitrary",),
                                             vmem_limit_bytes=VMEM_LIMIT),
        name="ada_mod",
    )(cv, w_ada, b_ada)


def _pre_kernel(x_ref, xp_ref, xn_ref, mod_ref, g1_ref, w_in_ref, mu_rkv_ref, mu_wag_ref,
                wd1_ref, wd2_ref, wd0_ref, a1_ref, a2_ref, a0_ref, kk_w_ref, rk_ref,
                seg_ref,
                pw0_ref, pw1_ref, r_ref, k_ref, v_ref, kk_ref, al0_ref, al1_ref,
                bonus_ref):
    d = x_ref.shape[-1]
    mod = mod_ref[...]
    shift, scale = mod[:, 0:d], mod[:, d:2 * d]
    h, h_prev, h_next = _tile_h(x_ref, xp_ref, xn_ref, g1_ref[...], scale, shift)
    h_up, h_dn = _shift_rows(h, h_prev, h_next)
    hd = 0.5 * (h_up + h_dn) - h

    d_pool = len(POOL_WINDOWS) * POOL_GROUP
    w_rkv = w_in_ref[:, d_pool:d_pool + 3 * d]
    tm = h.shape[0]
    halo = jnp.concatenate([h_prev, h_next, jnp.zeros((SUBLANES - 2, d), F32)], axis=0)
    z_all = _dot(jnp.concatenate([h, halo], axis=0).astype(BF16), w_rkv)
    z, zh = z_all[:tm], z_all[tm:]
    z_up, z_dn = _shift_rows(z, zh[0:1, :], zh[1:2, :])
    mu_rkv = mu_rkv_ref[...]
    z = z * (1.0 - mu_rkv) + (z_up + z_dn) * (0.5 * mu_rkv)
    r, k, v = z[:, 0:d], z[:, d:2 * d], z[:, 2 * d:3 * d]

    mu = mu_wag_ref[...]
    xw = h + hd * mu[0:1, :]
    xa = h + hd * mu[1:2, :]

    tw = jnp.tanh(_dot(xw.astype(BF16), wd1_ref[...]))
    pre_w = wd0_ref[...] + _dot(tw.astype(BF16), wd2_ref[...])
    ta = _dot(xa.astype(BF16), a1_ref[...])
    a_logit = a0_ref[...] + _dot(ta.astype(BF16), a2_ref[...])

    seg = seg_ref[...]
    kkr = k * kk_w_ref[...]
    kk = kkr * lax.rsqrt(_seg_sum(kkr * kkr, seg) + NORM_EPS)

    _store_pairs(pw0_ref, pre_w[:, 0:d])
    _store_pairs(pw1_ref, pre_w[:, d:2 * d])
    _store_pairs(r_ref, r)
    _store_pairs(k_ref, k)
    _store_pairs(v_ref, v)
    _store_pairs(kk_ref, kk)
    _store_pairs(al0_ref, a_logit[:, 0:d])
    _store_pairs(al1_ref, a_logit[:, d:2 * d])
    bonus_ref[...] = _seg_sum(r * k * rk_ref[...], seg) * v


def _store_pairs(ref, val):
    for q in range(ref.shape[0]):
        ref[q] = val[:, q * PAIR:(q + 1) * PAIR].astype(ref.dtype)


def _pair_spec(tm, n_pairs):
    return pl.BlockSpec((None, n_pairs, tm, PAIR), lambda b, j: (b, 0, j, 0))


def _const_spec(shape):
    nd = len(shape)
    return pl.BlockSpec(shape, lambda b, j: (0,) * nd, pipeline_mode=pl.Buffered(1))


def _token_specs(tm, seq, d):
    tiles8 = tm // SUBLANES
    last8 = seq // SUBLANES - 1
    x_spec = pl.BlockSpec((None, tm, d), lambda b, j: (b, j, 0))
    xp_spec = pl.BlockSpec((None, SUBLANES, d), lambda b, j: (b, jnp.maximum(j * tiles8 - 1, 0), 0))
    xn_spec = pl.BlockSpec((None, SUBLANES, d), lambda b, j: (b, jnp.minimum((j + 1) * tiles8, last8), 0))
    return x_spec, xp_spec, xn_spec


def _pre_call(x, mod, p, consts):
    bsz, seq, d = x.shape
    tm = TOKEN_TILE
    x_spec, xp_spec, xn_spec = _token_specs(tm, seq, d)
    mod_spec = pl.BlockSpec((None, 1, mod.shape[-1]), lambda b, j: (b % mod.shape[0], 0, 0))
    weights = (p["g1"], p["w_in"], p["mu_rkv"], p["mu_wag"], p["wd1"], p["wd2"], p["wd0"],
               p["a1"], p["a2"], p["a0"], p["k_k"], p["r_k"], consts["seg"])
    n_pairs = d // PAIR
    pm = lambda dt: jax.ShapeDtypeStruct((bsz, n_pairs, seq, PAIR), dt)
    return pl.pallas_call(
        _pre_kernel,
        out_shape=tuple(pm(dt) for dt in SCAN_OPERAND_DTYPES) + (jax.ShapeDtypeStruct((bsz, seq, d), F32),),
        grid=(bsz, seq // tm),
        in_specs=[x_spec, xp_spec, xn_spec, mod_spec] + [_const_spec(w.shape) for w in weights],
        out_specs=(_pair_spec(tm, n_pairs),) * len(SCAN_OPERAND_DTYPES) + (x_spec,),
        compiler_params=pltpu.CompilerParams(dimension_semantics=("parallel", "parallel"),
                                             vmem_limit_bytes=VMEM_LIMIT),
        name="pre_scan",
    )(x, x, x, mod, *weights)


def _stack(x, lane_head):
    zero = jnp.zeros_like(x)
    return jnp.concatenate([jnp.where(lane_head == 0, x, zero), jnp.where(lane_head == 1, x, zero)], axis=0)


def _chunk_prep(insts, cst):
    lane_head, bd = cst["lane_head"], cst["bd"]
    c = insts[0][0].shape[0]
    ids = range(len(insts))
    rev = [t[7] for t in insts]
    stk = lambda x: _stack(x, lane_head)
    lws = [-_sigmoid(t[0]) * float(np.exp(-0.5)) for t in insts]
    cum = [_dot_exact_lhs(cst["tri"][rev[i]], lws[i]) for i in ids]
    tot = [cum[i][(0 if rev[i] else c - 1):(1 if rev[i] else c), :] for i in ids]
    yield
    op = []
    for i in ids:
        r, k, v, kk = (t.astype(F32) for t in insts[i][1:5])
        lw = lws[i]
        a_gate = _sigmoid(insts[i][5])
        kd = k * (1.0 + (a_gate - 1.0) * insts[i][6])
        b = kk * a_gate
        e_in = jnp.exp(cum[i])
        e_ex = jnp.exp(cum[i] - lw)
        e_ng = jnp.exp(-cum[i])
        e_end = jnp.exp(tot[i] - cum[i])
        a = kk * e_ex
        op.append(dict(
            r=r * e_in,
            ar=jnp.concatenate([a, r * e_in], axis=0).astype(BF16),
            bk=jnp.concatenate([stk(b * e_ng), stk(kd * e_ng)], axis=0).astype(BF16),
            a_st=stk(a).astype(BF16), v_st=stk(v).astype(BF16), v=v.astype(BF16),
            bh=(-(b * e_end)).astype(BF16), bkh=jnp.concatenate([-(b * e_end), kd * e_end], axis=0).astype(BF16)))
    sc = [_dot_nt(op[i]["ar"], op[i]["bk"]) for i in ids]
    yield
    strict = [cst["strict"][rev[i]] for i in ids]
    incl = [cst["incl"][rev[i]] for i in ids]
    l_w = [jnp.where(strict[i], -sc[i][:c, :PAIR], 0.0) for i in ids]
    m_kk = [jnp.concatenate([jnp.where(strict[i], sc[i][:c, PAIR:], 0.0),
                             jnp.where(incl[i], sc[i][c:, PAIR:], 0.0)], axis=0).astype(BF16) for i in ids]
    m_rb = [jnp.where(incl[i], -sc[i][c:, :PAIR], 0.0).astype(BF16) for i in ids]
    mvk = [_dot(m_kk[i], op[i]["v_st"]) for i in ids]
    yield

    x_w = []
    for i in ids:
        d_w = jnp.where(cst["blk"][BASE_BLOCK], l_w[i], 0.0)
        x_w.append((cst["eye_w"] + d_w, d_w.astype(BF16)))
    p_w = [_dot(x_w[i][1], stk(x_w[i][1])) for i in ids]
    x_w = [x_w[i][0] for i in ids]
    yield
    n_sq = int(np.log2(BASE_BLOCK)) - 1
    for q in range(n_sq):
        p_bd = [stk(p_w[i].astype(BF16)) for i in ids]
        if q < n_sq - 1:
            zz = [_dot(jnp.concatenate([x_w[i], p_w[i]], axis=0).astype(BF16), p_bd[i]) for i in ids]
            x_w = [x_w[i] + zz[i][:c] for i in ids]
            p_w = [zz[i][c:] for i in ids]
        else:
            x_w = [x_w[i] + _dot(x_w[i].astype(BF16), p_bd[i]) for i in ids]
        yield
    m = BASE_BLOCK
    while m < c:
        f_bd = [stk(jnp.where(cst["blk"][2 * m] & ~cst["blk"][m], l_w[i], 0.0).astype(BF16)) for i in ids]
        t_b = [x_w[i].astype(BF16) for i in ids]
        u_w = [_dot(t_b[i], f_bd[i]) for i in ids]
        yield
        x_w = [x_w[i] + _dot(u_w[i].astype(BF16), stk(t_b[i])) for i in ids]
        yield
        m *= 2

    w_p = [_dot(x_w[i].astype(BF16), jnp.concatenate([op[i]["a_st"], stk(mvk[i][:c]).astype(BF16)], axis=1))
           for i in ids]
    yield
    w_st = [jnp.concatenate([stk(w_p[i][:, :PAIR]), stk(w_p[i][:, PAIR:])], axis=1).astype(BF16) for i in ids]
    ry = [_dot(m_rb[i], w_st[i]) for i in ids]
    g_f = [_dot_tn(op[i]["bh"], w_p[i][:, :PAIR].astype(BF16)) for i in ids]
    uv = [jnp.concatenate([w_p[i][:, PAIR:].astype(BF16), op[i]["v"]], axis=0) for i in ids]
    h_f = [_dot_tn(op[i]["bkh"], uv[i]) for i in ids]
    yield
    out = []
    for i in ids:
        r_p = op[i]["r"] + ry[i][:, :PAIR]
        g_m = jnp.where(cst["eye"], jnp.exp(tot[i]), 0.0) + jnp.where(bd, g_f[i], 0.0)
        h_bd = jnp.where(bd, h_f[i], 0.0)
        out.append((jnp.concatenate([r_p, g_m], axis=0).astype(BF16), ry[i][:, PAIR:] + mvk[i][c:],
                    h_bd[:c] + h_bd[c:]))
    return out


def _interleave(*gens):
    results = [None] * len(gens)
    live = dict(enumerate(gens))
    while live:
        for i in list(live):
            try:
                next(live[i])
            except StopIteration as done:
                results[i] = done.value
                del live[i]
    return results


def _scan_consts(c):
    n = 2 * c
    row = lax.broadcasted_iota(jnp.int32, (n, n), 0)
    col = lax.broadcasted_iota(jnp.int32, (n, n), 1)
    tr = lax.broadcasted_iota(jnp.int32, (c, c), 0)
    tc = lax.broadcasted_iota(jnp.int32, (c, c), 1)
    t = lax.broadcasted_iota(jnp.int32, (c, PAIR), 0)
    lane = lax.broadcasted_iota(jnp.int32, (c, PAIR), 1)
    s = lane % HEAD_DIM
    return {
        "tri": (jnp.where(tc <= tr, 1.0, 0.0).astype(BF16), jnp.where(tc >= tr, 1.0, 0.0).astype(BF16)),
        "strict": (s < t, s > t),
        "incl": (s <= t, s >= t),
        "eye_w": jnp.where(s == t, 1.0, 0.0),
        "blk": {m: (s // m) == (t // m) for m in (BASE_BLOCK << e for e in range(int(np.log2(c // BASE_BLOCK)) + 1))},
        "eye": row == col,
        "bd": (row // c) == (col // c),
        "lane_head": lane // HEAD_DIM,
    }


def _scan_kernel(*refs, has_s0, want_final):
    pw0_ref, pw1_ref, r_ref, k_ref, v_ref, kk_ref, al0_ref, al1_ref, ka_ref = refs[:9]
    pos = 9
    s0_ref = None
    if has_s0:
        s0_ref = refs[pos]
        pos += 1
    y_ref = refs[pos]
    pos += 1
    sf_ref = None
    if want_final:
        sf_ref = refs[pos]
        pos += 1
    yb_ref, st_ref, rg_ref, h_ref = refs[pos:pos + 4]

    c = CHUNK
    pp, seq = r_ref.shape[0], r_ref.shape[1]
    nc = seq // c
    ipt = min(ITEMS_PER_TRIP, pp * nc)
    cst = _scan_consts(c)
    zero = jnp.zeros((HEAD_DIM, HEAD_DIM), F32)
    pw_refs, al_refs, yv_refs = (pw0_ref, pw1_ref), (al0_ref, al1_ref), (y_ref, yb_ref)

    for q in range(pp):
        for dr in range(2):
            if has_s0:
                s_a = s0_ref[dr, 2 * q].T
                s_b = s0_ref[dr, 2 * q + 1].T
                st_ref[q, dr] = jnp.concatenate([jnp.concatenate([s_a, zero], axis=1),
                                                 jnp.concatenate([zero, s_b], axis=1)], axis=0)
            else:
                st_ref[q, dr] = jnp.zeros((PAIR, PAIR), F32)

    cps = min(nc, ipt)
    ppt = ipt // cps
    tpp = nc // cps
    n_trips = (pp * nc) // ipt

    def trip_chunks(t):
        g, k = t // tpp, t % tpp
        return [[(g * ppt + u, dr, (k * cps + j) if dr == 0 else nc - 1 - (k * cps + j))
                 for u in range(ppt) for dr in range(2)] for j in range(cps)]

    def prep(t):
        dest = [x for step in trip_chunks(t) for x in step]
        insts = []
        for q, dr, ck in dest:
            rows = pl.ds(pl.multiple_of(ck * c, c), c)
            insts.append((pw_refs[dr][q, rows, :], r_ref[q, rows, :], k_ref[q, rows, :], v_ref[q, rows, :],
                          kk_ref[q, rows, :], al_refs[dr][q, rows, :], ka_ref[q], dr))
        outs = yield from _chunk_prep(insts, cst)
        for (q, dr, ck), (rg, y_v, h_m) in zip(dest, outs):
            rg_ref[q, dr, ck] = rg
            h_ref[q, dr, ck] = h_m
            yv_refs[dr][q, pl.ds(pl.multiple_of(ck * c, c), c), :] = y_v

    def chain(t):
        for step in trip_chunks(t):
            z = [_dot(rg_ref[q, dr, ck], st_ref[q, dr].astype(BF16)) for q, dr, ck in step]
            for (q, dr, ck), z_i in zip(step, z):
                rows = pl.ds(pl.multiple_of(ck * c, c), c)
                yv_refs[dr][q, rows, :] = yv_refs[dr][q, rows, :] + z_i[:c]
                h_w = h_ref[q, dr, ck]
                st_ref[q, dr] = z_i[c:] + jnp.where(cst["bd"], jnp.concatenate([h_w, h_w], axis=0), 0.0)
            yield

    _interleave(prep(0))

    def trip_body(t, carry):
        _interleave(chain(t - 1), prep(t))
        return carry

    lax.fori_loop(1, n_trips, trip_body, 0)
    _interleave(chain(n_trips - 1))
    y_ref[...] = y_ref[...] + yb_ref[...]
    if want_final:
        for q in range(pp):
            for dr in range(2):
                s_bd = st_ref[q, dr]
                sf_ref[dr, 2 * q] = s_bd[:HEAD_DIM, :HEAD_DIM].T
                sf_ref[dr, 2 * q + 1] = s_bd[HEAD_DIM:, HEAD_DIM:].T


def _pairs_per_step(n_pairs, seq, ops):
    nc = seq // CHUNK
    io = (sum(o.dtype.itemsize for o in ops) + 4) * 2 * seq * PAIR
    scratch = 4 * seq * PAIR + 2 * nc * ((CHUNK + PAIR) * PAIR * 2 + CHUNK * PAIR * 4) + 2 * PAIR * PAIR * 4
    pp = 1
    while pp * 2 <= n_pairs and n_pairs % (pp * 2) == 0 and (pp * 2) * (io + scratch) <= SCAN_VMEM_BUDGET:
        pp *= 2
    return pp


def _scan_call(ops, k_a, s0, want_final):
    bsz, n_pairs, seq, _ = ops[0].shape
    nc = seq // CHUNK
    pp = _pairs_per_step(n_pairs, seq, ops)
    ipt = min(ITEMS_PER_TRIP, pp * nc)
    assert (pp * nc) % ipt == 0 and ipt % min(nc, ipt) == 0 and nc % min(nc, ipt) == 0
    col_spec = pl.BlockSpec((None, pp, seq, PAIR), lambda b, g: (b, g, 0, 0))
    st_spec = pl.BlockSpec((None, 2, 2 * pp, HEAD_DIM, HEAD_DIM), lambda b, g: (b, 0, g, 0, 0))
    in_specs = [col_spec] * len(ops) + [pl.BlockSpec((pp, 1, PAIR), lambda b, g: (g, 0, 0))]
    args = list(ops) + [k_a]
    has_s0 = s0 is not None
    if has_s0:
        in_specs.append(st_spec)
        args.append(s0)
    out_shape = [jax.ShapeDtypeStruct((bsz, n_pairs, seq, PAIR), F32)]
    out_specs = [col_spec]
    if want_final:
        out_shape.append(jax.ShapeDtypeStruct((bsz, 2, N_HEADS, HEAD_DIM, HEAD_DIM), F32))
        out_specs.append(st_spec)
    res = pl.pallas_call(
        functools.partial(_scan_kernel, has_s0=has_s0, want_final=want_final),
        out_shape=tuple(out_shape),
        grid=(bsz, n_pairs // pp),
        in_specs=in_specs,
        out_specs=tuple(out_specs),
        scratch_shapes=[pltpu.VMEM((pp, seq, PAIR), F32), pltpu.VMEM((pp, 2, PAIR, PAIR), F32),
                        pltpu.VMEM((pp, 2, nc, CHUNK + PAIR, PAIR), BF16), pltpu.VMEM((pp, 2, nc, CHUNK, PAIR), F32)],
        compiler_params=pltpu.CompilerParams(dimension_semantics=("parallel", "parallel"),
                                             vmem_limit_bytes=VMEM_LIMIT),
        name="wkv_scan",
    )(*args)
    return res if want_final else (res[0], None)


def _post_kernel(x_ref, xp_ref, xn_ref, mod_ref, y_ref, bonus_ref, g1_ref, g2_ref, gf_ref,
                 w_in_ref, mu_wag_ref, gw1_ref, gw2_ref, lnw_ref, lnb_ref, pool_w_ref, pool_scale_ref,
                 band_ref, icnt_ref, seg_ref, w_out_ref, w_ff1_ref, w_ff2_ref, o_ref, *, final_norm):
    d = x_ref.shape[-1]
    d_pool = len(POOL_WINDOWS) * POOL_GROUP
    mod = mod_ref[...]
    sh1, sc1, ga1, sh2, sc2, ga2 = (mod[:, i * d:(i + 1) * d] for i in range(N_MOD))
    x = x_ref[...]
    h, h_prev, h_next = _tile_h(x_ref, xp_ref, xn_ref, g1_ref[...], sc1, sh1)
    h_up, h_dn = _shift_rows(h, h_prev, h_next)
    xg = h + (0.5 * (h_up + h_dn) - h) * mu_wag_ref[2:3, :]

    hb = h.astype(BF16)
    z = _dot(hb, w_in_ref[:, 0:d_pool])
    zg = _dot(hb, w_in_ref[:, d_pool + 3 * d:d_pool + 5 * d])
    outs = []
    for g in range(len(POOL_WINDOWS)):
        zp = z[:, g * POOL_GROUP:(g + 1) * POOL_GROUP]
        zh, zl = _split2(zp)
        band = band_ref[g]
        mixed = (_dot(band, zh) + _dot(band, zl)) * icnt_ref[g] - zp
        outs.append(_dot(mixed.astype(BF16), pool_w_ref[g]))
    a_out = jnp.concatenate(outs, axis=-1) * pool_scale_ref[...]
    gate_a = _sigmoid(zg[:, 0:d])
    gate_b = _sigmoid(zg[:, d:2 * d])
    gl = _sigmoid(_dot(xg.astype(BF16), gw1_ref[...]))
    g_out = _dot(gl.astype(BF16), gw2_ref[...])

    seg = seg_ref[...]
    y = jnp.concatenate([y_ref[q] for q in range(y_ref.shape[0])], axis=-1)
    yc = y - _seg_sum(y, seg) * (1.0 / HEAD_DIM)
    var = _seg_sum(yc * yc, seg) * (1.0 / HEAD_DIM)
    yn = yc * lax.rsqrt(var + LNX_EPS) * lnw_ref[...] + lnb_ref[...]
    b_out = (yn + bonus_ref[...]) * g_out
    mix = _dot((gate_a * a_out + gate_b * b_out).astype(BF16), w_out_ref[...])
    x = x + ga1 * mix

    h2 = _modulated_norm(x, g2_ref[...], sc2, sh2)
    u = jnp.maximum(_dot(h2.astype(BF16), w_ff1_ref[...]), 0.0)
    ff = _dot((u * u).astype(BF16), w_ff2_ref[...])
    x = x + ga2 * ff
    if final_norm:
        x = x * lax.rsqrt(jnp.mean(x * x, axis=-1, keepdims=True) + RMS_EPS) * gf_ref[...]
    o_ref[...] = x


def _post_call(x, mod, y, bonus, p, consts, g_final, final_norm):
    bsz, seq, d = x.shape
    tm = TOKEN_TILE
    x_spec, xp_spec, xn_spec = _token_specs(tm, seq, d)
    mod_spec = pl.BlockSpec((None, 1, mod.shape[-1]), lambda b, j: (b % mod.shape[0], 0, 0))
    weights = (p["g1"], p["g2"], g_final, p["w_in"], p["mu_wag"], p["gw1"], p["gw2"], p["ln_x_w"],
               p["ln_x_b"], p["pool_w"], p["pool_scale"], consts["band"], consts["icnt"], consts["seg"],
               p["w_out"], p["w_ff1"], p["w_ff2"])
    return pl.pallas_call(
        functools.partial(_post_kernel, final_norm=final_norm),
        out_shape=jax.ShapeDtypeStruct((bsz, seq, d), F32),
        grid=(bsz, seq // tm),
        in_specs=[x_spec, xp_spec, xn_spec, mod_spec, _pair_spec(tm, d // PAIR), x_spec]
        + [_const_spec(w.shape) for w in weights],
        out_specs=x_spec,
        compiler_params=pltpu.CompilerParams(dimension_semantics=("parallel", "parallel"),
                                             vmem_limit_bytes=VMEM_LIMIT),
        name="post_scan",
    )(x, x, x, mod, y, bonus, *weights)


def _pool_consts(tm, row_len):
    t = np.arange(tm)
    pos, row = t % row_len, t // row_len
    bands, icnts = [], []
    for win in POOL_WINDOWS:
        lo = np.clip(pos - win // 2, 0, row_len)
        hi = np.clip(pos + win - win // 2, 0, row_len)
        inside = (pos[None, :] >= lo[:, None]) & (pos[None, :] < hi[:, None]) & (row[None, :] == row[:, None])
        bands.append(inside.astype(np.float32))
        icnts.append(np.broadcast_to((1.0 / (hi - lo))[:, None], (tm, POOL_GROUP)).astype(np.float32))
    return jnp.asarray(np.stack(bands), BF16), jnp.asarray(np.stack(icnts), F32)


def _block_diag2(w):
    z = jnp.zeros_like(w[0])
    return jnp.concatenate([jnp.concatenate([w[0], z], axis=1), jnp.concatenate([z, w[1]], axis=1)], axis=0)


def _layer_params(li, w):
    row = lambda a: a.reshape(1, -1)
    return {
        "g1": row(w["g_norm1"][li]), "g2": row(w["g_norm2"][li]),
        "w_in": w["w_in"][li].astype(BF16),
        "mu_rkv": row(w["mu_rkv"][li]), "mu_wag": w["mu_wag"][li],
        "wd1": jnp.concatenate([w["w_dec1"][li, 0], w["w_dec1"][li, 1]], axis=1).astype(BF16),
        "wd2": _block_diag2(w["w_dec2"][li]).astype(BF16), "wd0": row(w["w_dec0"][li]),
        "a1": jnp.concatenate([w["a1"][li, 0], w["a1"][li, 1]], axis=1).astype(BF16),
        "a2": _block_diag2(w["a2"][li]).astype(BF16), "a0": row(w["a0"][li]),
        "gw1": w["gate_w1"][li].astype(BF16), "gw2": w["gate_w2"][li].astype(BF16),
        "k_k": row(w["k_k"][li]), "k_a": row(w["k_a"][li]), "r_k": row(w["r_k"][li]),
        "ln_x_w": row(w["ln_x_w"][li]), "ln_x_b": row(w["ln_x_b"][li]),
        "pool_w": w["pool_w"][li].astype(BF16), "pool_scale": row(w["pool_scale"][li]),
        "w_out": w["w_out"][li].astype(BF16), "w_ff1": w["w_ff1"][li].astype(BF16),
        "w_ff2": w["w_ff2"][li].astype(BF16),
    }


def _layer(x, mod, s0, want_final, p, consts, g_final, final_norm):
    ops = _pre_call(x, mod, p, consts)
    y, s_fin = _scan_call(ops[:-1], p["k_a"].reshape(-1, 1, PAIR), s0, want_final)
    return _post_call(x, mod, y, ops[-1], p, consts, g_final, final_norm), s_fin


def kernel(x_prompt, x_sample, state_rwkv, c, c_ctx, w_ada, b_ada, g_norm1, g_norm2, w_in, mu_rkv, mu_wag, w_dec0, w_dec1, w_dec2, a0, a1, a2, gate_w1, gate_w2, k_k, k_a, r_k, ln_x_w, ln_x_b, pool_w, pool_scale, w_out, w_ff1, w_ff2, g_final):
    w = dict(w_ada=w_ada, b_ada=b_ada, g_norm1=g_norm1, g_norm2=g_norm2, w_in=w_in, mu_rkv=mu_rkv,
             mu_wag=mu_wag, w_dec0=w_dec0, w_dec1=w_dec1, w_dec2=w_dec2, a0=a0, a1=a1, a2=a2,
             gate_w1=gate_w1, gate_w2=gate_w2, k_k=k_k, k_a=k_a, r_k=r_k, ln_x_w=ln_x_w, ln_x_b=ln_x_b,
             pool_w=pool_w, pool_scale=pool_scale, w_out=w_out, w_ff1=w_ff1, w_ff2=w_ff2)
    depth = w_in.shape[0]
    n_lat, lat_len, d = x_sample.shape
    ctx_len = x_prompt.shape[1]
    assert ctx_len % TOKEN_TILE == 0 and lat_len % TOKEN_TILE == 0 and TOKEN_TILE % GRID_W == 0
    assert ctx_len == TOKEN_TILE, "context pooling runs over the whole sequence inside one token tile"

    seg = jnp.asarray(np.kron(np.eye(SEG_TILE // HEAD_DIM), np.ones((HEAD_DIM, HEAD_DIM))), BF16)
    band_c, icnt_c = _pool_consts(TOKEN_TILE, ctx_len)
    band_l, icnt_l = _pool_consts(TOKEN_TILE, GRID_W)
    consts_ctx = {"seg": seg, "band": band_c, "icnt": icnt_c}
    consts_lat = {"seg": seg, "band": band_l, "icnt": icnt_l}

    n_cv = 1 + n_lat
    cv = jnp.concatenate([c_ctx[None, :], c, jnp.zeros((-n_cv % SUBLANES, d), F32)], axis=0)
    g_fin = g_final.reshape(1, d)
    ctx, lat = x_prompt, x_sample
    new_states = []
    for li in range(depth):
        p = _layer_params(li, w)
        mod = _ada_call(cv, w_ada[li], b_ada[li].reshape(1, -1))
        mod_ctx = mod[0:1].reshape(1, 1, -1)
        mod_lat = mod[1:n_cv].reshape(n_lat, 1, -1)
        last = li == depth - 1
        ctx, s_fin = _layer(ctx, mod_ctx, None, True, p, consts_ctx, g_fin, last)
        new_states.append(s_fin)
        lat, _ = _layer(lat, mod_lat, state_rwkv[:, li], False, p, consts_lat, g_fin, last)
    return ctx, lat, jnp.stack(new_states, axis=1)
```

```python
import functools

import numpy as np
import jax
import jax.numpy as jnp
from jax import lax
from jax.experimental import pallas as pl
from jax.experimental.pallas import tpu as pltpu

F32 = jnp.float32
BF16 = jnp.bfloat16

N_HEADS = 16
HEAD_DIM = 64
PAIR = 2 * HEAD_DIM
POOL_WINDOWS = (2, 4, 8, 16)
POOL_GROUP = 128
GRID_W = 64
N_MOD = 6
RMS_EPS = 1e-6
LNX_EPS = 64e-5
NORM_EPS = 1e-12

CHUNK = 64
SCAN_OPERAND_DTYPES = (F32, F32, BF16, F32, BF16, F32, F32, F32)
BASE_BLOCK = 8
ITEMS_PER_TRIP = 8
SCAN_VMEM_BUDGET = 50 * 1024 * 1024
TOKEN_TILE = 256
SEG_TILE = 256
SUBLANES = 8
VMEM_LIMIT = 58 * 1024 * 1024


def _dot(a, b):
    return jnp.dot(a, b, preferred_element_type=F32)


def _dot_nt(a, b):
    return lax.dot_general(a, b, (((1,), (1,)), ((), ())), preferred_element_type=F32)


def _dot_tn(a, b):
    return lax.dot_general(a, b, (((0,), (0,)), ((), ())), preferred_element_type=F32)


def _split2(x):
    hi = x.astype(BF16)
    return hi, (x - hi.astype(F32)).astype(BF16)


def _dot_exact_rhs(x, m):
    hi, lo = _split2(x)
    return _dot(hi, m) + _dot(lo, m)


def _dot_exact_lhs(m, x):
    hi, lo = _split2(x)
    return _dot(m, hi) + _dot(m, lo)


def _seg_sum(x, seg_ones):
    d = x.shape[-1]
    parts = [_dot_exact_rhs(x[:, q:q + SEG_TILE], seg_ones) for q in range(0, d, SEG_TILE)]
    return jnp.concatenate(parts, axis=-1)


def _sigmoid(x):
    return 1.0 / (1.0 + jnp.exp(-x))


def _modulated_norm(x, g, scale, shift):
    y = x * lax.rsqrt(jnp.mean(x * x, axis=-1, keepdims=True) + RMS_EPS)
    return (y * g) * (1.0 + scale) + shift


def _shift_rows(v, first_row, last_row):
    n = v.shape[0]
    row = lax.broadcasted_iota(jnp.int32, v.shape, 0)
    up = jnp.where(row == 0, first_row, pltpu.roll(v, 1, 0))
    dn = jnp.where(row == n - 1, last_row, pltpu.roll(v, n - 1, 0))
    return up, dn


def _tile_h(x_ref, xp_ref, xn_ref, g, scale, shift):
    j = pl.program_id(1)
    nj = pl.num_programs(1)
    h = _modulated_norm(x_ref[...], g, scale, shift)
    halo = jnp.concatenate([xp_ref[SUBLANES - 1:SUBLANES, :], xn_ref[0:1, :]], axis=0)
    hh = _modulated_norm(halo, g, scale, shift)
    h_prev = jnp.where(j > 0, hh[0:1, :], 0.0)
    h_next = jnp.where(j < nj - 1, hh[1:2, :], 0.0)
    return h, h_prev, h_next


def _ada_kernel(cv_ref, w_ref, b_ref, o_ref):
    cv = cv_ref[...]
    hi, lo = _split2(cv * _sigmoid(cv))
    w = w_ref[...].astype(BF16)
    o_ref[...] = _dot(hi, w) + _dot(lo, w) + b_ref[...]


def _ada_call(cv, w_ada, b_ada):
    rows, d = cv.shape
    n = w_ada.shape[1]
    bn = n // 4
    return pl.pallas_call(
        _ada_kernel,
        out_shape=jax.ShapeDtypeStruct((rows, n), F32),
        grid=(n // bn,),
        in_specs=[pl.BlockSpec((rows, d), lambda i: (0, 0)),
                  pl.BlockSpec((d, bn), lambda i: (0, i)),
                  pl.BlockSpec((1, bn), lambda i: (0, i))],
        out_specs=pl.BlockSpec((rows, bn), lambda i: (0, i)),
        compiler_params=pltpu.CompilerParams(dimension_semantics=("arbitrary",),
                                             vmem_limit_bytes=VMEM_LIMIT),
        name="ada_mod",
    )(cv, w_ada, b_ada)


def _pre_kernel(x_ref, xp_ref, xn_ref, mod_ref, g1_ref, w_in_ref, mu_rkv_ref, mu_wag_ref,
                wd1_ref, wd2_ref, wd0_ref, a1_ref, a2_ref, a0_ref, kk_w_ref, rk_ref,
                seg_ref,
                pw0_ref, pw1_ref, r_ref, k_ref, v_ref, kk_ref, al0_ref, al1_ref,
                bonus_ref):
    d = x_ref.shape[-1]
    mod = mod_ref[...]
    shift, scale = mod[:, 0:d], mod[:, d:2 * d]
    h, h_prev, h_next = _tile_h(x_ref, xp_ref, xn_ref, g1_ref[...], scale, shift)
    h_up, h_dn = _shift_rows(h, h_prev, h_next)
    hd = 0.5 * (h_up + h_dn) - h

    d_pool = len(POOL_WINDOWS) * POOL_GROUP
    w_rkv = w_in_ref[:, d_pool:d_pool + 3 * d]
    tm = h.shape[0]
    halo = jnp.concatenate([h_prev, h_next, jnp.zeros((SUBLANES - 2, d), F32)], axis=0)
    z_all = _dot(jnp.concatenate([h, halo], axis=0).astype(BF16), w_rkv)
    z, zh = z_all[:tm], z_all[tm:]
    z_up, z_dn = _shift_rows(z, zh[0:1, :], zh[1:2, :])
    mu_rkv = mu_rkv_ref[...]
    z = z * (1.0 - mu_rkv) + (z_up + z_dn) * (0.5 * mu_rkv)
    r, k, v = z[:, 0:d], z[:, d:2 * d], z[:, 2 * d:3 * d]

    mu = mu_wag_ref[...]
    xw = h + hd * mu[0:1, :]
    xa = h + hd * mu[1:2, :]

    tw = jnp.tanh(_dot(xw.astype(BF16), wd1_ref[...]))
    pre_w = wd0_ref[...] + _dot(tw.astype(BF16), wd2_ref[...])
    ta = _dot(xa.astype(BF16), a1_ref[...])
    a_logit = a0_ref[...] + _dot(ta.astype(BF16), a2_ref[...])

    seg = seg_ref[...]
    kkr = k * kk_w_ref[...]
    kk = kkr * lax.rsqrt(_seg_sum(kkr * kkr, seg) + NORM_EPS)

    _store_pairs(pw0_ref, pre_w[:, 0:d])
    _store_pairs(pw1_ref, pre_w[:, d:2 * d])
    _store_pairs(r_ref, r)
    _store_pairs(k_ref, k)
    _store_pairs(v_ref, v)
    _store_pairs(kk_ref, kk)
    _store_pairs(al0_ref, a_logit[:, 0:d])
    _store_pairs(al1_ref, a_logit[:, d:2 * d])
    bonus_ref[...] = _seg_sum(r * k * rk_ref[...], seg) * v


def _store_pairs(ref, val):
    for q in range(ref.shape[0]):
        ref[q] = val[:, q * PAIR:(q + 1) * PAIR].astype(ref.dtype)


def _pair_spec(tm, n_pairs):
    return pl.BlockSpec((None, n_pairs, tm, PAIR), lambda b, j: (b, 0, j, 0))


def _const_spec(shape):
    nd = len(shape)
    return pl.BlockSpec(shape, lambda b, j: (0,) * nd, pipeline_mode=pl.Buffered(1))


def _token_specs(tm, seq, d):
    tiles8 = tm // SUBLANES
    last8 = seq // SUBLANES - 1
    x_spec = pl.BlockSpec((None, tm, d), lambda b, j: (b, j, 0))
    xp_spec = pl.BlockSpec((None, SUBLANES, d), lambda b, j: (b, jnp.maximum(j * tiles8 - 1, 0), 0))
    xn_spec = pl.BlockSpec((None, SUBLANES, d), lambda b, j: (b, jnp.minimum((j + 1) * tiles8, last8), 0))
    return x_spec, xp_spec, xn_spec


def _pre_call(x, mod, p, consts):
    bsz, seq, d = x.shape
    tm = TOKEN_TILE
    x_spec, xp_spec, xn_spec = _token_specs(tm, seq, d)
    mod_spec = pl.BlockSpec((None, 1, mod.shape[-1]), lambda b, j: (b % mod.shape[0], 0, 0))
    weights = (p["g1"], p["w_in"], p["mu_rkv"], p["mu_wag"], p["wd1"], p["wd2"], p["wd0"],
               p["a1"], p["a2"], p["a0"], p["k_k"], p["r_k"], consts["seg"])
    n_pairs = d // PAIR
    pm = lambda dt: jax.ShapeDtypeStruct((bsz, n_pairs, seq, PAIR), dt)
    return pl.pallas_call(
        _pre_kernel,
        out_shape=tuple(pm(dt) for dt in SCAN_OPERAND_DTYPES) + (jax.ShapeDtypeStruct((bsz, seq, d), F32),),
        grid=(bsz, seq // tm),
        in_specs=[x_spec, xp_spec, xn_spec, mod_spec] + [_const_spec(w.shape) for w in weights],
        out_specs=(_pair_spec(tm, n_pairs),) * len(SCAN_OPERAND_DTYPES) + (x_spec,),
        compiler_params=pltpu.CompilerParams(dimension_semantics=("parallel", "parallel"),
                                             vmem_limit_bytes=VMEM_LIMIT),
        name="pre_scan",
    )(x, x, x, mod, *weights)


def _stack(x, lane_head):
    zero = jnp.zeros_like(x)
    return jnp.concatenate([jnp.where(lane_head == 0, x, zero), jnp.where(lane_head == 1, x, zero)], axis=0)


def _chunk_prep(insts, cst):
    lane_head, bd = cst["lane_head"], cst["bd"]
    c = insts[0][0].shape[0]
    ids = range(len(insts))
    rev = [t[7] for t in insts]
    stk = lambda x: _stack(x, lane_head)
    lws = [-_sigmoid(t[0]) * float(np.exp(-0.5)) for t in insts]
    cum = [_dot_exact_lhs(cst["tri"][rev[i]], lws[i]) for i in ids]
    tot = [cum[i][(0 if rev[i] else c - 1):(1 if rev[i] else c), :] for i in ids]
    yield
    op = []
    for i in ids:
        r, k, v, kk = (t.astype(F32) for t in insts[i][1:5])
        lw = lws[i]
        a_gate = _sigmoid(insts[i][5])
        kd = k * (1.0 + (a_gate - 1.0) * insts[i][6])
        b = kk * a_gate
        e_in = jnp.exp(cum[i])
        e_ex = jnp.exp(cum[i] - lw)
        e_ng = jnp.exp(-cum[i])
        e_end = jnp.exp(tot[i] - cum[i])
        a = kk * e_ex
        op.append(dict(
            r=r * e_in,
            ar=jnp.concatenate([a, r * e_in], axis=0).astype(BF16),
            bk=jnp.concatenate([stk(b * e_ng), stk(kd * e_ng)], axis=0).astype(BF16),
            a_st=stk(a).astype(BF16), v_st=stk(v).astype(BF16), v=v.astype(BF16),
            bh=(-(b * e_end)).astype(BF16), bkh=jnp.concatenate([-(b * e_end), kd * e_end], axis=0).astype(BF16)))
    sc = [_dot_nt(op[i]["ar"], op[i]["bk"]) for i in ids]
    yield
    strict = [cst["strict"][rev[i]] for i in ids]
    incl = [cst["incl"][rev[i]] for i in ids]
    l_w = [jnp.where(strict[i], -sc[i][:c, :PAIR], 0.0) for i in ids]
    m_kk = [jnp.concatenate([jnp.where(strict[i], sc[i][:c, PAIR:], 0.0),
                             jnp.where(incl[i], sc[i][c:, PAIR:], 0.0)], axis=0).astype(BF16) for i in ids]
    m_rb = [jnp.where(incl[i], -sc[i][c:, :PAIR], 0.0).astype(BF16) for i in ids]
    mvk = [_dot(m_kk[i], op[i]["v_st"]) for i in ids]
    yield

    x_w = []
    for i in ids:
        d_w = jnp.where(cst["blk"][BASE_BLOCK], l_w[i], 0.0)
        x_w.append((cst["eye_w"] + d_w, d_w.astype(BF16)))
    p_w = [_dot(x_w[i][1], stk(x_w[i][1])) for i in ids]
    x_w = [x_w[i][0] for i in ids]
    yield
    n_sq = int(np.log2(BASE_BLOCK)) - 1
    for q in range(n_sq):
        p_bd = [stk(p_w[i].astype(BF16)) for i in ids]
        if q < n_sq - 1:
            zz = [_dot(jnp.concatenate([x_w[i], p_w[i]], axis=0).astype(BF16), p_bd[i]) for i in ids]
            x_w = [x_w[i] + zz[i][:c] for i in ids]
            p_w = [zz[i][c:] for i in ids]
        else:
            x_w = [x_w[i] + _dot(x_w[i].astype(BF16), p_bd[i]) for i in ids]
        yield
    m = BASE_BLOCK
    while m < c:
        f_bd = [stk(jnp.where(cst["blk"][2 * m] & ~cst["blk"][m], l_w[i], 0.0).astype(BF16)) for i in ids]
        t_b = [x_w[i].astype(BF16) for i in ids]
        u_w = [_dot(t_b[i], f_bd[i]) for i in ids]
        yield
        x_w = [x_w[i] + _dot(u_w[i].astype(BF16), stk(t_b[i])) for i in ids]
        yield
        m *= 2

    w_p = [_dot(x_w[i].astype(BF16), jnp.concatenate([op[i]["a_st"], stk(mvk[i][:c]).astype(BF16)], axis=1))
           for i in ids]
    yield
    w_st = [jnp.concatenate([stk(w_p[i][:, :PAIR]), stk(w_p[i][:, PAIR:])], axis=1).astype(BF16) for i in ids]
    ry = [_dot(m_rb[i], w_st[i]) for i in ids]
    g_f = [_dot_tn(op[i]["bh"], w_p[i][:, :PAIR].astype(BF16)) for i in ids]
    uv = [jnp.concatenate([w_p[i][:, PAIR:].astype(BF16), op[i]["v"]], axis=0) for i in ids]
    h_f = [_dot_tn(op[i]["bkh"], uv[i]) for i in ids]
    yield
    out = []
    for i in ids:
        r_p = op[i]["r"] + ry[i][:, :PAIR]
        g_m = jnp.where(cst["eye"], jnp.exp(tot[i]), 0.0) + jnp.where(bd, g_f[i], 0.0)
        h_bd = jnp.where(bd, h_f[i], 0.0)
        out.append((jnp.concatenate([r_p, g_m], axis=0).astype(BF16), ry[i][:, PAIR:] + mvk[i][c:],
                    h_bd[:c] + h_bd[c:]))
    return out


def _interleave(*gens):
    results = [None] * len(gens)
    live = dict(enumerate(gens))
    while live:
        for i in list(live):
            try:
                next(live[i])
            except StopIteration as done:
                results[i] = done.value
                del live[i]
    return results


def _scan_consts(c):
    n = 2 * c
    row = lax.broadcasted_iota(jnp.int32, (n, n), 0)
    col = lax.broadcasted_iota(jnp.int32, (n, n), 1)
    tr = lax.broadcasted_iota(jnp.int32, (c, c), 0)
    tc = lax.broadcasted_iota(jnp.int32, (c, c), 1)
    t = lax.broadcasted_iota(jnp.int32, (c, PAIR), 0)
    lane = lax.broadcasted_iota(jnp.int32, (c, PAIR), 1)
    s = lane % HEAD_DIM
    return {
        "tri": (jnp.where(tc <= tr, 1.0, 0.0).astype(BF16), jnp.where(tc >= tr, 1.0, 0.0).astype(BF16)),
        "strict": (s < t, s > t),
        "incl": (s <= t, s >= t),
        "eye_w": jnp.where(s == t, 1.0, 0.0),
        "blk": {m: (s // m) == (t // m) for m in (BASE_BLOCK << e for e in range(int(np.log2(c // BASE_BLOCK)) + 1))},
        "eye": row == col,
        "bd": (row // c) == (col // c),
        "lane_head": lane // HEAD_DIM,
    }


def _scan_kernel(*refs, has_s0, want_final):
    pw0_ref, pw1_ref, r_ref, k_ref, v_ref, kk_ref, al0_ref, al1_ref, ka_ref = refs[:9]
    pos = 9
    s0_ref = None
    if has_s0:
        s0_ref = refs[pos]
        pos += 1
    y_ref = refs[pos]
    pos += 1
    sf_ref = None
    if want_final:
        sf_ref = refs[pos]
        pos += 1
    yb_ref, st_ref, rg_ref, h_ref = refs[pos:pos + 4]

    c = CHUNK
    pp, seq = r_ref.shape[0], r_ref.shape[1]
    nc = seq // c
    ipt = min(ITEMS_PER_TRIP, pp * nc)
    cst = _scan_consts(c)
    zero = jnp.zeros((HEAD_DIM, HEAD_DIM), F32)
    pw_refs, al_refs, yv_refs = (pw0_ref, pw1_ref), (al0_ref, al1_ref), (y_ref, yb_ref)

    for q in range(pp):
        for dr in range(2):
            if has_s0:
                s_a = s0_ref[dr, 2 * q].T
                s_b = s0_ref[dr, 2 * q + 1].T
                st_ref[q, dr] = jnp.concatenate([jnp.concatenate([s_a, zero], axis=1),
                                                 jnp.concatenate([zero, s_b], axis=1)], axis=0)
            else:
                st_ref[q, dr] = jnp.zeros((PAIR, PAIR), F32)

    cps = min(nc, ipt)
    ppt = ipt // cps
    tpp = nc // cps
    n_trips = (pp * nc) // ipt

    def trip_chunks(t):
        g, k = t // tpp, t % tpp
        return [[(g * ppt + u, dr, (k * cps + j) if dr == 0 else nc - 1 - (k * cps + j))
                 for u in range(ppt) for dr in range(2)] for j in range(cps)]

    def prep(t):
        dest = [x for step in trip_chunks(t) for x in step]
        insts = []
        for q, dr, ck in dest:
            rows = pl.ds(pl.multiple_of(ck * c, c), c)
            insts.append((pw_refs[dr][q, rows, :], r_ref[q, rows, :], k_ref[q, rows, :], v_ref[q, rows, :],
                          kk_ref[q, rows, :], al_refs[dr][q, rows, :], ka_ref[q], dr))
        outs = yield from _chunk_prep(insts, cst)
        for (q, dr, ck), (rg, y_v, h_m) in zip(dest, outs):
            rg_ref[q, dr, ck] = rg
            h_ref[q, dr, ck] = h_m
            yv_refs[dr][q, pl.ds(pl.multiple_of(ck * c, c), c), :] = y_v

    def chain(t):
        for step in trip_chunks(t):
            z = [_dot(rg_ref[q, dr, ck], st_ref[q, dr].astype(BF16)) for q, dr, ck in step]
            for (q, dr, ck), z_i in zip(step, z):
                rows = pl.ds(pl.multiple_of(ck * c, c), c)
                yv_refs[dr][q, rows, :] = yv_refs[dr][q, rows, :] + z_i[:c]
                h_w = h_ref[q, dr, ck]
                st_ref[q, dr] = z_i[c:] + jnp.where(cst["bd"], jnp.concatenate([h_w, h_w], axis=0), 0.0)
            yield

    _interleave(prep(0))

    def trip_body(t, carry):
        _interleave(chain(t - 1), prep(t))
        return carry

    lax.fori_loop(1, n_trips, trip_body, 0)
    _interleave(chain(n_trips - 1))
    y_ref[...] = y_ref[...] + yb_ref[...]
    if want_final:
        for q in range(pp):
            for dr in range(2):
                s_bd = st_ref[q, dr]
                sf_ref[dr, 2 * q] = s_bd[:HEAD_DIM, :HEAD_DIM].T
                sf_ref[dr, 2 * q + 1] = s_bd[HEAD_DIM:, HEAD_DIM:].T


def _pairs_per_step(n_pairs, seq, ops):
    nc = seq // CHUNK
    io = (sum(o.dtype.itemsize for o in ops) + 4) * 2 * seq * PAIR
    scratch = 4 * seq * PAIR + 2 * nc * ((CHUNK + PAIR) * PAIR * 2 + CHUNK * PAIR * 4) + 2 * PAIR * PAIR * 4
    pp = 1
    while pp * 2 <= n_pairs and n_pairs % (pp * 2) == 0 and (pp * 2) * (io + scratch) <= SCAN_VMEM_BUDGET:
        pp *= 2
    return pp


def _scan_call(ops, k_a, s0, want_final):
    bsz, n_pairs, seq, _ = ops[0].shape
    nc = seq // CHUNK
    pp = _pairs_per_step(n_pairs, seq, ops)
    ipt = min(ITEMS_PER_TRIP, pp * nc)
    assert (pp * nc) % ipt == 0 and ipt % min(nc, ipt) == 0 and nc % min(nc, ipt) == 0
    col_spec = pl.BlockSpec((None, pp, seq, PAIR), lambda b, g: (b, g, 0, 0))
    st_spec = pl.BlockSpec((None, 2, 2 * pp, HEAD_DIM, HEAD_DIM), lambda b, g: (b, 0, g, 0, 0))
    in_specs = [col_spec] * len(ops) + [pl.BlockSpec((pp, 1, PAIR), lambda b, g: (g, 0, 0))]
    args = list(ops) + [k_a]
    has_s0 = s0 is not None
    if has_s0:
        in_specs.append(st_spec)
        args.append(s0)
    out_shape = [jax.ShapeDtypeStruct((bsz, n_pairs, seq, PAIR), F32)]
    out_specs = [col_spec]
    if want_final:
        out_shape.append(jax.ShapeDtypeStruct((bsz, 2, N_HEADS, HEAD_DIM, HEAD_DIM), F32))
        out_specs.append(st_spec)
    res = pl.pallas_call(
        functools.partial(_scan_kernel, has_s0=has_s0, want_final=want_final),
        out_shape=tuple(out_shape),
        grid=(bsz, n_pairs // pp),
        in_specs=in_specs,
        out_specs=tuple(out_specs),
        scratch_shapes=[pltpu.VMEM((pp, seq, PAIR), F32), pltpu.VMEM((pp, 2, PAIR, PAIR), F32),
                        pltpu.VMEM((pp, 2, nc, CHUNK + PAIR, PAIR), BF16), pltpu.VMEM((pp, 2, nc, CHUNK, PAIR), F32)],
        compiler_params=pltpu.CompilerParams(dimension_semantics=("parallel", "parallel"),
                                             vmem_limit_bytes=VMEM_LIMIT),
        name="wkv_scan",
    )(*args)
    return res if want_final else (res[0], None)


def _post_kernel(x_ref, xp_ref, xn_ref, mod_ref, y_ref, bonus_ref, g1_ref, g2_ref, gf_ref,
                 w_in_ref, mu_wag_ref, gw1_ref, gw2_ref, lnw_ref, lnb_ref, pool_w_ref, pool_scale_ref,
                 band_ref, icnt_ref, seg_ref, w_out_ref, w_ff1_ref, w_ff2_ref, o_ref, *, final_norm):
    d = x_ref.shape[-1]
    d_pool = len(POOL_WINDOWS) * POOL_GROUP
    mod = mod_ref[...]
    sh1, sc1, ga1, sh2, sc2, ga2 = (mod[:, i * d:(i + 1) * d] for i in range(N_MOD))
    x = x_ref[...]
    h, h_prev, h_next = _tile_h(x_ref, xp_ref, xn_ref, g1_ref[...], sc1, sh1)
    h_up, h_dn = _shift_rows(h, h_prev, h_next)
    xg = h + (0.5 * (h_up + h_dn) - h) * mu_wag_ref[2:3, :]

    hb = h.astype(BF16)
    z = _dot(hb, w_in_ref[:, 0:d_pool])
    zg = _dot(hb, w_in_ref[:, d_pool + 3 * d:d_pool + 5 * d])
    outs = []
    for g in range(len(POOL_WINDOWS)):
        zp = z[:, g * POOL_GROUP:(g + 1) * POOL_GROUP]
        zh, zl = _split2(zp)
        band = band_ref[g]
        mixed = (_dot(band, zh) + _dot(band, zl)) * icnt_ref[g] - zp
        outs.append(_dot(mixed.astype(BF16), pool_w_ref[g]))
    a_out = jnp.concatenate(outs, axis=-1) * pool_scale_ref[...]
    gate_a = _sigmoid(zg[:, 0:d])
    gate_b = _sigmoid(zg[:, d:2 * d])
    gl = _sigmoid(_dot(xg.astype(BF16), gw1_ref[...]))
    g_out = _dot(gl.astype(BF16), gw2_ref[...])

    seg = seg_ref[...]
    y = jnp.concatenate([y_ref[q] for q in range(y_ref.shape[0])], axis=-1)
    yc = y - _seg_sum(y, seg) * (1.0 / HEAD_DIM)
    var = _seg_sum(yc * yc, seg) * (1.0 / HEAD_DIM)
    yn = yc * lax.rsqrt(var + LNX_EPS) * lnw_ref[...] + lnb_ref[...]
    b_out = (yn + bonus_ref[...]) * g_out
    mix = _dot((gate_a * a_out + gate_b * b_out).astype(BF16), w_out_ref[...])
    x = x + ga1 * mix

    h2 = _modulated_norm(x, g2_ref[...], sc2, sh2)
    u = jnp.maximum(_dot(h2.astype(BF16), w_ff1_ref[...]), 0.0)
    ff = _dot((u * u).astype(BF16), w_ff2_ref[...])
    x = x + ga2 * ff
    if final_norm:
        x = x * lax.rsqrt(jnp.mean(x * x, axis=-1, keepdims=True) + RMS_EPS) * gf_ref[...]
    o_ref[...] = x


def _post_call(x, mod, y, bonus, p, consts, g_final, final_norm):
    bsz, seq, d = x.shape
    tm = TOKEN_TILE
    x_spec, xp_spec, xn_spec = _token_specs(tm, seq, d)
    mod_spec = pl.BlockSpec((None, 1, mod.shape[-1]), lambda b, j: (b % mod.shape[0], 0, 0))
    weights = (p["g1"], p["g2"], g_final, p["w_in"], p["mu_wag"], p["gw1"], p["gw2"], p["ln_x_w"],
               p["ln_x_b"], p["pool_w"], p["pool_scale"], consts["band"], consts["icnt"], consts["seg"],
               p["w_out"], p["w_ff1"], p["w_ff2"])
    return pl.pallas_call(
        functools.partial(_post_kernel, final_norm=final_norm),
        out_shape=jax.ShapeDtypeStruct((bsz, seq, d), F32),
        grid=(bsz, seq // tm),
        in_specs=[x_spec, xp_spec, xn_spec, mod_spec, _pair_spec(tm, d // PAIR), x_spec]
        + [_const_spec(w.shape) for w in weights],
        out_specs=x_spec,
        compiler_params=pltpu.CompilerParams(dimension_semantics=("parallel", "parallel"),
                                             vmem_limit_bytes=VMEM_LIMIT),
        name="post_scan",
    )(x, x, x, mod, y, bonus, *weights)


def _pool_consts(tm, row_len):
    t = np.arange(tm)
    pos, row = t % row_len, t // row_len
    bands, icnts = [], []
    for win in POOL_WINDOWS:
        lo = np.clip(pos - win // 2, 0, row_len)
        hi = np.clip(pos + win - win // 2, 0, row_len)
        inside = (pos[None, :] >= lo[:, None]) & (pos[None, :] < hi[:, None]) & (row[None, :] == row[:, None])
        bands.append(inside.astype(np.float32))
        icnts.append(np.broadcast_to((1.0 / (hi - lo))[:, None], (tm, POOL_GROUP)).astype(np.float32))
    return jnp.asarray(np.stack(bands), BF16), jnp.asarray(np.stack(icnts), F32)


def _block_diag2(w):
    z = jnp.zeros_like(w[0])
    return jnp.concatenate([jnp.concatenate([w[0], z], axis=1), jnp.concatenate([z, w[1]], axis=1)], axis=0)


def _layer_params(li, w):
    row = lambda a: a.reshape(1, -1)
    return {
        "g1": row(w["g_norm1"][li]), "g2": row(w["g_norm2"][li]),
        "w_in": w["w_in"][li].astype(BF16),
        "mu_rkv": row(w["mu_rkv"][li]), "mu_wag": w["mu_wag"][li],
        "wd1": jnp.concatenate([w["w_dec1"][li, 0], w["w_dec1"][li, 1]], axis=1).astype(BF16),
        "wd2": _block_diag2(w["w_dec2"][li]).astype(BF16), "wd0": row(w["w_dec0"][li]),
        "a1": jnp.concatenate([w["a1"][li, 0], w["a1"][li, 1]], axis=1).astype(BF16),
        "a2": _block_diag2(w["a2"][li]).astype(BF16), "a0": row(w["a0"][li]),
        "gw1": w["gate_w1"][li].astype(BF16), "gw2": w["gate_w2"][li].astype(BF16),
        "k_k": row(w["k_k"][li]), "k_a": row(w["k_a"][li]), "r_k": row(w["r_k"][li]),
        "ln_x_w": row(w["ln_x_w"][li]), "ln_x_b": row(w["ln_x_b"][li]),
        "pool_w": w["pool_w"][li].astype(BF16), "pool_scale": row(w["pool_scale"][li]),
        "w_out": w["w_out"][li].astype(BF16), "w_ff1": w["w_ff1"][li].astype(BF16),
        "w_ff2": w["w_ff2"][li].astype(BF16),
    }


def _layer(x, mod, s0, want_final, p, consts, g_final, final_norm):
    ops = _pre_call(x, mod, p, consts)
    y, s_fin = _scan_call(ops[:-1], p["k_a"].reshape(-1, 1, PAIR), s0, want_final)
    return _post_call(x, mod, y, ops[-1], p, consts, g_final, final_norm), s_fin


def kernel(x_prompt, x_sample, state_rwkv, c, c_ctx, w_ada, b_ada, g_norm1, g_norm2, w_in, mu_rkv, mu_wag, w_dec0, w_dec1, w_dec2, a0, a1, a2, gate_w1, gate_w2, k_k, k_a, r_k, ln_x_w, ln_x_b, pool_w, pool_scale, w_out, w_ff1, w_ff2, g_final):
    w = dict(w_ada=w_ada, b_ada=b_ada, g_norm1=g_norm1, g_norm2=g_norm2, w_in=w_in, mu_rkv=mu_rkv,
             mu_wag=mu_wag, w_dec0=w_dec0, w_dec1=w_dec1, w_dec2=w_dec2, a0=a0, a1=a1, a2=a2,
             gate_w1=gate_w1, gate_w2=gate_w2, k_k=k_k, k_a=k_a, r_k=r_k, ln_x_w=ln_x_w, ln_x_b=ln_x_b,
             pool_w=pool_w, pool_scale=pool_scale, w_out=w_out, w_ff1=w_ff1, w_ff2=w_ff2)
    depth = w_in.shape[0]
    n_lat, lat_len, d = x_sample.shape
    ctx_len = x_prompt.shape[1]
    assert ctx_len % TOKEN_TILE == 0 and lat_len % TOKEN_TILE == 0 and TOKEN_TILE % GRID_W == 0
    assert ctx_len == TOKEN_TILE, "context pooling runs over the whole sequence inside one token tile"

    seg = jnp.asarray(np.kron(np.eye(SEG_TILE // HEAD_DIM), np.ones((HEAD_DIM, HEAD_DIM))), BF16)
    band_c, icnt_c = _pool_consts(TOKEN_TILE, ctx_len)
    band_l, icnt_l = _pool_consts(TOKEN_TILE, GRID_W)
    consts_ctx = {"seg": seg, "band": band_c, "icnt": icnt_c}
    consts_lat = {"seg": seg, "band": band_l, "icnt": icnt_l}

    n_cv = 1 + n_lat
    cv = jnp.concatenate([c_ctx[None, :], c, jnp.zeros((-n_cv % SUBLANES, d), F32)], axis=0)
    g_fin = g_final.reshape(1, d)
    ctx, lat = x_prompt, x_sample
    new_states = []
    for li in range(depth):
        p = _layer_params(li, w)
        mod = _ada_call(cv, w_ada[li], b_ada[li].reshape(1, -1))
        mod_ctx = mod[0:1].reshape(1, 1, -1)
        mod_lat = mod[1:n_cv].reshape(n_lat, 1, -1)
        last = li == depth - 1
        ctx, s_fin = _layer(ctx, mod_ctx, None, True, p, consts_ctx, g_fin, last)
        new_states.append(s_fin)
        lat, _ = _layer(lat, mod_lat, state_rwkv[:, li], False, p, consts_lat, g_fin, last)
    return ctx, lat, jnp.stack(new_states, axis=1)
```
